```python
import jax, jax.numpy as jnp
from jax import lax
import numpy as np

D_MODEL = 2048
BATCH = 8
SEQ = 2048
DEPTH = 2

HEAD_DIM = 128
ATTN_WIDTH = D_MODEL // 2
N_Q_HEADS = ATTN_WIDTH // HEAD_DIM
N_KV_HEADS = max(1, N_Q_HEADS // 4)
GQA_GROUP = N_Q_HEADS // N_KV_HEADS
KV_WIDTH = N_KV_HEADS * HEAD_DIM
WINDOW = 128
BLOCK = 128
GMLP_WIDTH = D_MODEL - ATTN_WIDTH
GMLP_HEAD_DIM = 128
N_GMLP_HEADS = GMLP_WIDTH // GMLP_HEAD_DIM
CHUNK = 128
IN_WIDTH = ATTN_WIDTH + 2 * KV_WIDTH + 2 * GMLP_WIDTH
D_FF = 5632
CONV_WIDTH = 3
ROPE_THETA = 10000.0
EPS = 1e-6
MASK_VALUE = -1e30

kernel_name = "hybrid_window_gqa_sgu_convffn_encoder"


def rms_norm(x, g):
    xf = x.astype(jnp.float32)
    y = xf * lax.rsqrt(jnp.mean(xf * xf, axis=-1, keepdims=True) + EPS)
    return (y * g.astype(jnp.float32)).astype(x.dtype)


def layer_norm(x, g, b):
    xf = x.astype(jnp.float32)
    mu = jnp.mean(xf, axis=-1, keepdims=True)
    xc = xf - mu
    y = xc * lax.rsqrt(jnp.mean(xc * xc, axis=-1, keepdims=True) + EPS)
    return (y * g.astype(jnp.float32) + b.astype(jnp.float32)).astype(x.dtype)


def rope_tables(seq):
    inv_freq = ROPE_THETA ** (-jnp.arange(0, HEAD_DIM, 2, dtype=jnp.float32) / HEAD_DIM)
    ang = jnp.arange(seq, dtype=jnp.float32)[:, None] * inv_freq[None, :]
    return jnp.cos(ang), jnp.sin(ang)


def apply_rope(x, cos, sin):
    xf = x.astype(jnp.float32)
    x1, x2 = jnp.split(xf, 2, axis=-1)
    c = cos[None, :, None, :]
    s = sin[None, :, None, :]
    return jnp.concatenate([x1 * c - x2 * s, x2 * c + x1 * s], axis=-1).astype(x.dtype)


def banded_window_attention(q, k, v, sink):
    B, S, _, D = q.shape
    nb = S // BLOCK
    qb = q.reshape(B, nb, BLOCK, N_KV_HEADS, GQA_GROUP, D)

    def band(t):
        tp = jnp.pad(t, ((0, 0), (BLOCK, BLOCK), (0, 0), (0, 0)))
        tp = tp.reshape(B, nb + 2, BLOCK, N_KV_HEADS, D)
        return jnp.concatenate([tp[:, :-2], tp[:, 1:-1], tp[:, 2:]], axis=2)

    kb, vb = band(k), band(v)
    s = jnp.einsum('bnqhgd,bnkhd->bnhgqk', qb, kb).astype(jnp.float32) * (D ** -0.5)
    blk = jnp.arange(nb)[:, None, None]
    q_pos = blk * BLOCK + jnp.arange(BLOCK)[None, :, None]
    k_pos = blk * BLOCK - BLOCK + jnp.arange(3 * BLOCK)[None, None, :]
    valid = (jnp.abs(k_pos - q_pos) <= WINDOW) & (k_pos >= 0) & (k_pos < S)
    s = jnp.where(valid[None, :, None, None], s, MASK_VALUE)
    sk = sink.astype(jnp.float32).reshape(N_KV_HEADS, GQA_GROUP)[None, None, :, :, None, None]
    m = jnp.maximum(jnp.max(s, axis=-1, keepdims=True), sk)
    p = jnp.exp(s - m)
    probs = p / (jnp.sum(p, axis=-1, keepdims=True) + jnp.exp(sk - m))
    out = jnp.einsum('bnhgqk,bnkhd->bnqhgd', probs.astype(v.dtype), vb)
    return out.reshape(B, S, N_Q_HEADS * D)


def chunked_spatial_gating(u, v, ln_g, ln_b, w_s, b_s):
    B, S, _ = u.shape
    nc = S // CHUNK
    vn = layer_norm(v, ln_g, ln_b).reshape(B, nc, CHUNK, N_GMLP_HEADS, GMLP_HEAD_DIM)
    f = jnp.einsum('hpq,bcqhd->bcphd', w_s, vn) + b_s.T[None, None, :, :, None]
    return u * f.reshape(B, S, GMLP_WIDTH)


def depthwise_conv_centred(h, w, b):
    S = h.shape[1]
    half = CONV_WIDTH // 2
    hp = jnp.pad(h, ((0, 0), (half, half), (0, 0)))
    out = b
    for t in range(CONV_WIDTH):
        out = out + hp[:, t:t + S] * w[t]
    return out


def conv_gated_ffn(h, w_up, conv_w, conv_b, w_down):
    a = depthwise_conv_centred(h @ w_up, conv_w, conv_b)
    g, u = jnp.split(a, 2, axis=-1)
    return (jax.nn.silu(g) * u) @ w_down


def setup_inputs(seed: int = 0) -> dict:
    key = jax.random.key(seed)
    ks = jax.random.split(key, 20)
    f32 = jnp.float32
    nrm = lambda k, shape, scale: jax.random.normal(k, shape, f32) * scale
    res_scale = (2.0 * DEPTH) ** -0.5
    return {
        "x": jax.random.normal(ks[0], (BATCH, SEQ, D_MODEL), f32),
        "norm1_g": 1.0 + nrm(ks[1], (DEPTH, D_MODEL), 0.02),
        "w_in": nrm(ks[2], (DEPTH, D_MODEL, IN_WIDTH), D_MODEL ** -0.5),
        "q_norm_g": 1.0 + nrm(ks[3], (DEPTH, HEAD_DIM), 0.02),
        "k_norm_g": 1.0 + nrm(ks[4], (DEPTH, HEAD_DIM), 0.02),
        "sink": nrm(ks[5], (DEPTH, N_Q_HEADS), 0.5),
        "sgu_ln_g": 1.0 + nrm(ks[6], (DEPTH, GMLP_WIDTH), 0.02),
        "sgu_ln_b": nrm(ks[7], (DEPTH, GMLP_WIDTH), 0.02),
        "w_s": nrm(ks[8], (DEPTH, N_GMLP_HEADS, CHUNK, CHUNK), 0.5 * CHUNK ** -0.5),
        "b_s": 1.0 + nrm(ks[9], (DEPTH, N_GMLP_HEADS, CHUNK), 0.02),
        "attn_out_g": 1.0 + nrm(ks[10], (DEPTH, ATTN_WIDTH), 0.02),
        "sgu_out_g": 1.0 + nrm(ks[11], (DEPTH, GMLP_WIDTH), 0.02),
        "w_o": nrm(ks[12], (DEPTH, D_MODEL, D_MODEL), D_MODEL ** -0.5 * res_scale),
        "norm2_g": 1.0 + nrm(ks[13], (DEPTH, D_MODEL), 0.02),
        "w_up": nrm(ks[14], (DEPTH, D_MODEL, 2 * D_FF), D_MODEL ** -0.5),
        "conv_w": nrm(ks[15], (DEPTH, CONV_WIDTH, 2 * D_FF), CONV_WIDTH ** -0.5),
        "conv_b": nrm(ks[16], (DEPTH, 2 * D_FF), 0.01),
        "w_down": nrm(ks[17], (DEPTH, D_FF, D_MODEL), D_FF ** -0.5 * res_scale),
    }


def reference(x, norm1_g, w_in, q_norm_g, k_norm_g, sink, sgu_ln_g, sgu_ln_b, w_s, b_s,
              attn_out_g, sgu_out_g, w_o, norm2_g, w_up, conv_w, conv_b, w_down):
    B, S, _ = x.shape
    cos, sin = rope_tables(S)
    splits = [ATTN_WIDTH, ATTN_WIDTH + KV_WIDTH, ATTN_WIDTH + 2 * KV_WIDTH,
              ATTN_WIDTH + 2 * KV_WIDTH + GMLP_WIDTH]
    for l in range(DEPTH):
        h = rms_norm(x, norm1_g[l])
        q, k, v, gu, gv = jnp.split(h @ w_in[l], splits, axis=-1)
        q = apply_rope(rms_norm(q.reshape(B, S, N_Q_HEADS, HEAD_DIM), q_norm_g[l]), cos, sin)
        k = apply_rope(rms_norm(k.reshape(B, S, N_KV_HEADS, HEAD_DIM), k_norm_g[l]), cos, sin)
        v = v.reshape(B, S, N_KV_HEADS, HEAD_DIM)
        attn = banded_window_attention(q, k, v, sink[l])
        sgu = chunked_spatial_gating(jax.nn.gelu(gu), jax.nn.gelu(gv),
                                     sgu_ln_g[l], sgu_ln_b[l], w_s[l], b_s[l])
        mixed = jnp.concatenate([rms_norm(attn, attn_out_g[l]), rms_norm(sgu, sgu_out_g[l])], axis=-1)
        x = x + mixed @ w_o[l]
        x = x + conv_gated_ffn(rms_norm(x, norm2_g[l]), w_up[l], conv_w[l], conv_b[l], w_down[l])
    return x
```

```python
import functools

import jax
import jax.numpy as jnp
import numpy as np
from jax import lax
from jax.experimental import pallas as pl
from jax.experimental.pallas import tpu as pltpu

HEAD_DIM = 128
GQA_GROUP = 4
WINDOW = 128
BLOCK = 128
BAND = 3 * BLOCK
CONV_WIDTH = 3
ROPE_THETA = 10000.0
EPS = 1e-6
MASK_VALUE = -1e30

HALO = 16
VMEM_LIMIT_BYTES = 56 * 1024 * 1024

IN_TM = 512
MIX_TM = 512
FFN_TM = 512
FFN_TF = 512

_BF16 = jnp.bfloat16
_F32 = jnp.float32


def _dot(a, b):
    return jnp.dot(a, b, preferred_element_type=_F32)


def _rms(x, g):
    return x * lax.rsqrt(jnp.mean(x * x, axis=-1, keepdims=True) + EPS) * g


def _gelu_tanh(x):
    c = np.float32(np.sqrt(2.0 / np.pi))
    return x * (0.5 * (1.0 + jnp.tanh(c * (x + np.float32(0.044715) * (x * x * x)))))


def _resident(shape):
    zeros = (0,) * len(shape)
    return pl.BlockSpec(shape, lambda *_: zeros, pipeline_mode=pl.Buffered(1))


def _in_proj_body(x_ref, g1_ref, win_ref, qg_ref, kg_ref, cos_ref, sin_ref, lng_ref, lnb_ref,
                  q_ref, k_ref, v_ref, u_ref, vn_ref, *, attn_w, kv_w, gmlp_w):
    h = _rms(x_ref[...], g1_ref[...]).astype(_BF16)
    cosf = cos_ref[...]
    sinf = sin_ref[...]

    def qk_heads(col0, n_heads, gain, out_ref):
        y = _dot(h, win_ref[:, col0:col0 + n_heads * HEAD_DIM])
        for hd in range(n_heads):
            sl = slice(hd * HEAD_DIM, (hd + 1) * HEAD_DIM)
            t = _rms(y[:, sl], gain)
            out_ref[:, sl] = (t * cosf + pltpu.roll(t, HEAD_DIM // 2, 1) * sinf).astype(_BF16)

    qk_heads(0, attn_w // HEAD_DIM, qg_ref[...], q_ref)
    qk_heads(attn_w, kv_w // HEAD_DIM, kg_ref[...], k_ref)
    c = attn_w + kv_w
    v_ref[...] = _dot(h, win_ref[:, c:c + kv_w]).astype(_BF16)
    c += kv_w
    u_ref[...] = _gelu_tanh(_dot(h, win_ref[:, c:c + gmlp_w])).astype(_BF16)
    c += gmlp_w
    gv = _gelu_tanh(_dot(h, win_ref[:, c:c + gmlp_w]))
    xc = gv - jnp.mean(gv, axis=-1, keepdims=True)
    vn = xc * lax.rsqrt(jnp.mean(xc * xc, axis=-1, keepdims=True) + EPS) * lng_ref[...] + lnb_ref[...]
    vn_ref[...] = vn.astype(_BF16)


def _in_proj(x2, g1, win, qg, kg, cosf, sinf, lng, lnb, *, seq, attn_w, kv_w, gmlp_w):
    tokens, d_model = x2.shape
    tm = IN_TM
    tiles_per_seq = seq // tm
    row = lambda w: pl.BlockSpec((tm, w), lambda m: (m, 0))
    pos = pl.BlockSpec((tm, HEAD_DIM), lambda m: (m % tiles_per_seq, 0))
    return pl.pallas_call(
        functools.partial(_in_proj_body, attn_w=attn_w, kv_w=kv_w, gmlp_w=gmlp_w),
        grid=(tokens // tm,),
        in_specs=[row(d_model), _resident(g1.shape), _resident(win.shape), _resident(qg.shape),
                  _resident(kg.shape), pos, pos, _resident(lng.shape), _resident(lnb.shape)],
        out_specs=[row(attn_w), row(kv_w), row(kv_w), row(gmlp_w), row(gmlp_w)],
        out_shape=[jax.ShapeDtypeStruct((tokens, w), _BF16) for w in (attn_w, kv_w, kv_w, gmlp_w, gmlp_w)],
        compiler_params=pltpu.CompilerParams(dimension_semantics=("parallel",),
                                             vmem_limit_bytes=VMEM_LIMIT_BYTES),
        name="in_proj",
    )(x2, g1, win, qg, kg, cosf, sinf, lng, lnb)


def _mixer_body(sink_ref, q_ref, k_ref, v_ref, u_ref, vn_ref, x_ref, wo_ref, ws_ref, bst_ref,
                ag_ref, sg_ref, g2_ref, xo_ref, hn_ref, attn_scr, sgu_scr, mix_scr,
                *, seq, n_kv, n_gmlp):
    tm = q_ref.shape[0]
    attn_w = q_ref.shape[1]
    blocks = tm // BLOCK
    first_block = pl.program_id(1) * blocks
    scale = np.float32(HEAD_DIM ** -0.5)
    rows = GQA_GROUP * BLOCK

    def block(i, carry):
        r0 = pl.multiple_of(i * BLOCK, BLOCK)
        n = first_block + i
        ks = pl.multiple_of(jnp.clip((n - 1) * BLOCK, 0, seq - BAND), BLOCK)
        q_pos = n * BLOCK + (lax.broadcasted_iota(jnp.int32, (rows, BAND), 0) & (BLOCK - 1))
        k_pos = ks + lax.broadcasted_iota(jnp.int32, (rows, BAND), 1)
        valid = jnp.abs(k_pos - q_pos) <= WINDOW
        for g in range(n_kv):
            kv_sl = slice(g * HEAD_DIM, (g + 1) * HEAD_DIM)
            kb = k_ref[pl.ds(ks, BAND), kv_sl]
            vb = v_ref[pl.ds(ks, BAND), kv_sl]
            heads = range(g * GQA_GROUP, (g + 1) * GQA_GROUP)
            q4 = jnp.concatenate(
                [q_ref[pl.ds(r0, BLOCK), hd * HEAD_DIM:(hd + 1) * HEAD_DIM] for hd in heads], axis=0)
            s = lax.dot_general(q4, kb, (((1,), (1,)), ((), ())), preferred_element_type=_F32) * scale
            s = jnp.where(valid, s, MASK_VALUE)
            sk = jnp.concatenate([jnp.full((BLOCK, 1), sink_ref[hd], _F32) for hd in heads], axis=0)
            m = jnp.maximum(jnp.max(s, axis=-1, keepdims=True), sk)
            p = jnp.exp(s - m)
            denom = jnp.sum(p, axis=-1, keepdims=True) + jnp.exp(sk - m)
            o = _dot(p.astype(_BF16), vb) / denom
            for j, hd in enumerate(heads):
                attn_scr[pl.ds(r0, BLOCK), hd * HEAD_DIM:(hd + 1) * HEAD_DIM] = o[j * BLOCK:(j + 1) * BLOCK]
        for hd in range(n_gmlp):
            sl = slice(hd * HEAD_DIM, (hd + 1) * HEAD_DIM)
            f = _dot(ws_ref[hd], vn_ref[pl.ds(r0, BLOCK), sl]) + bst_ref[:, hd:hd + 1]
            sgu_scr[pl.ds(r0, BLOCK), sl] = u_ref[pl.ds(r0, BLOCK), sl].astype(_F32) * f
        return carry

    lax.fori_loop(0, blocks, block, 0)

    mix_scr[:, :attn_w] = _rms(attn_scr[...], ag_ref[...]).astype(_BF16)
    mix_scr[:, attn_w:] = _rms(sgu_scr[...], sg_ref[...]).astype(_BF16)
    x_new = x_ref[...] + _dot(mix_scr[...], wo_ref[...])
    xo_ref[...] = x_new
    hn_ref[...] = _rms(x_new, g2_ref[...]).astype(_BF16)


def _mixer(sink, q, k, v, u, vn, x2, wo, ws, bst, ag, sg, g2, *, batch, seq):
    tokens, d_model = x2.shape
    attn_w, kv_w, gmlp_w = q.shape[1], k.shape[1], u.shape[1]
    tm = MIX_TM
    tiles = seq // tm
    row = lambda w: pl.BlockSpec((tm, w), lambda b, t: (b * tiles + t, 0))
    whole_seq = pl.BlockSpec((seq, kv_w), lambda b, t: (b, 0))
    body = functools.partial(_mixer_body, seq=seq, n_kv=kv_w // HEAD_DIM, n_gmlp=gmlp_w // HEAD_DIM)
    return pl.pallas_call(
        body,
        grid=(batch, tiles),
        in_specs=[pl.BlockSpec(memory_space=pltpu.SMEM), row(attn_w), whole_seq, whole_seq, row(gmlp_w),
                  row(gmlp_w), row(d_model), _resident(wo.shape), _resident(ws.shape), _resident(bst.shape),
                  _resident(ag.shape), _resident(sg.shape), _resident(g2.shape)],
        out_specs=[row(d_model), row(d_model)],
        out_shape=[jax.ShapeDtypeStruct((tokens, d_model), _F32),
                   jax.ShapeDtypeStruct((tokens, d_model), _BF16)],
        scratch_shapes=[pltpu.VMEM((tm, attn_w), _F32), pltpu.VMEM((tm, gmlp_w), _F32),
                        pltpu.VMEM((tm, d_model), _BF16)],
        compiler_params=pltpu.CompilerParams(dimension_semantics=("parallel", "parallel"),
                                             vmem_limit_bytes=VMEM_LIMIT_BYTES),
        name="mixer",
    )(sink, q, k, v, u, vn, x2, wo, ws, bst, ag, sg, g2)


def _ffn_body(prev_ref, main_ref, next_ref, x_ref, wg_ref, wu_ref, cwg_ref, cwu_ref, cbg_ref, cbu_ref,
              wd_ref, o_ref, win_scr, *, tiles_per_seq):
    tm = main_ref.shape[0]
    m = pl.program_id(0)
    j = pl.program_id(1)

    @pl.when(j == 0)
    def _():
        t = m % tiles_per_seq
        win_scr[:HALO] = jnp.where(t == 0, jnp.zeros_like(prev_ref[...]), prev_ref[...])
        win_scr[HALO:HALO + tm] = main_ref[...]
        win_scr[HALO + tm:] = jnp.where(t == tiles_per_seq - 1, jnp.zeros_like(next_ref[...]), next_ref[...])
        o_ref[...] = x_ref[...]

    win = win_scr[...]
    wrows = tm + 2 * HALO

    def conv(w_ref, cw_ref, cb_ref):
        a = _dot(win, w_ref[...])
        below = pltpu.roll(a, 1, 0)
        above = pltpu.roll(a, wrows - 1, 0)
        c = cb_ref[...] + below * cw_ref[0:1, :] + a * cw_ref[1:2, :] + above * cw_ref[2:3, :]
        return c[HALO:HALO + tm]

    gate = conv(wg_ref, cwg_ref, cbg_ref)
    up = conv(wu_ref, cwu_ref, cbu_ref)
    act = (gate * (1.0 / (1.0 + jnp.exp(-gate))) * up).astype(_BF16)
    o_ref[...] += _dot(act, wd_ref[...])


def _ffn(hn, x2, w_up, conv_w, conv_b, w_down, *, seq):
    tokens, d_model = x2.shape
    d_ff = w_down.shape[0]
    tm, tf = FFN_TM, FFN_TF
    tiles_per_seq = seq // tm
    halo_blocks = tm // HALO
    last_halo = tokens // HALO - 1
    u_off = d_ff // tf
    row = lambda: pl.BlockSpec((tm, d_model), lambda m, j: (m, 0))
    return pl.pallas_call(
        functools.partial(_ffn_body, tiles_per_seq=tiles_per_seq),
        grid=(tokens // tm, d_ff // tf),
        in_specs=[
            pl.BlockSpec((HALO, d_model), lambda m, j: (jnp.maximum(m * halo_blocks - 1, 0), 0)),
            row(),
            pl.BlockSpec((HALO, d_model), lambda m, j: (jnp.minimum((m + 1) * halo_blocks, last_halo), 0)),
            row(),
            pl.BlockSpec((d_model, tf), lambda m, j: (0, j)),
            pl.BlockSpec((d_model, tf), lambda m, j: (0, j + u_off)),
            pl.BlockSpec((CONV_WIDTH, tf), lambda m, j: (0, j)),
            pl.BlockSpec((CONV_WIDTH, tf), lambda m, j: (0, j + u_off)),
            pl.BlockSpec((1, tf), lambda m, j: (0, j)),
            pl.BlockSpec((1, tf), lambda m, j: (0, j + u_off)),
            pl.BlockSpec((tf, d_model), lambda m, j: (j, 0)),
        ],
        out_specs=row(),
        out_shape=jax.ShapeDtypeStruct((tokens, d_model), _F32),
        scratch_shapes=[pltpu.VMEM((tm + 2 * HALO, d_model), _BF16)],
        compiler_params=pltpu.CompilerParams(dimension_semantics=("parallel", "arbitrary"),
                                             vmem_limit_bytes=VMEM_LIMIT_BYTES),
        name="ffn",
    )(hn, hn, hn, x2, w_up, w_up, conv_w, conv_w, conv_b, conv_b, w_down)


def _rope_tables(seq):
    inv_freq = ROPE_THETA ** (-jnp.arange(0, HEAD_DIM, 2, dtype=_F32) / HEAD_DIM)
    ang = jnp.arange(seq, dtype=_F32)[:, None] * inv_freq[None, :]
    cos, sin = jnp.cos(ang), jnp.sin(ang)
    return jnp.concatenate([cos, cos], axis=-1), jnp.concatenate([-sin, sin], axis=-1)


def kernel(x, norm1_g, w_in, q_norm_g, k_norm_g, sink, sgu_ln_g, sgu_ln_b, w_s, b_s,
           attn_out_g, sgu_out_g, w_o, norm2_g, w_up, conv_w, conv_b, w_down):
    batch, seq, d_model = x.shape
    depth = w_in.shape[0]
    attn_w = attn_out_g.shape[1]
    gmlp_w = sgu_out_g.shape[1]
    kv_w = (w_in.shape[2] - attn_w - 2 * gmlp_w) // 2
    assert seq % max(IN_TM, MIX_TM, FFN_TM) == 0 and w_down.shape[1] % FFN_TF == 0
    assert w_s.shape[2] == BLOCK and seq >= BAND

    cosf, sinf = _rope_tables(seq)
    row = lambda a: a.reshape(1, -1)
    x2 = x.reshape(batch * seq, d_model)
    for l in range(depth):
        q, k, v, u, vn = _in_proj(
            x2, row(norm1_g[l]), w_in[l].astype(_BF16), row(q_norm_g[l]), row(k_norm_g[l]), cosf, sinf,
            row(sgu_ln_g[l]), row(sgu_ln_b[l]), seq=seq, attn_w=attn_w, kv_w=kv_w, gmlp_w=gmlp_w)
        x2, hn = _mixer(
            sink[l], q, k, v, u, vn, x2, w_o[l].astype(_BF16), w_s[l].astype(_BF16), b_s[l].T,
            row(attn_out_g[l]), row(sgu_out_g[l]), row(norm2_g[l]), batch=batch, seq=seq)
        x2 = _ffn(hn, x2, w_up[l].astype(_BF16), conv_w[l], row(conv_b[l]), w_down[l].astype(_BF16), seq=seq)
    return x2.reshape(batch, seq, d_model)
```

```python
import functools

import jax
import jax.numpy as jnp
import numpy as np
from jax import lax
from jax.experimental import pallas as pl
from jax.experimental.pallas import tpu as pltpu

HEAD_DIM = 128
GQA_GROUP = 4
WINDOW = 128
BLOCK = 128
BAND = 3 * BLOCK
CONV_WIDTH = 3
ROPE_THETA = 10000.0
EPS = 1e-6
MASK_VALUE = -1e30

HALO = 16
VMEM_LIMIT_BYTES = 56 * 1024 * 1024

IN_TM = 512
MIX_TM = 512
MIX_GROUP_BLOCKS = 2
FFN_TM = 1024
FFN_TF = 512

_BF16 = jnp.bfloat16
_F32 = jnp.float32


def _dot(a, b):
    return jnp.dot(a, b, preferred_element_type=_F32)


def _rms(x, g):
    return x * lax.rsqrt(jnp.mean(x * x, axis=-1, keepdims=True) + EPS) * g


def _gelu_tanh(x):
    c = np.float32(np.sqrt(2.0 / np.pi))
    return x * (0.5 * (1.0 + jnp.tanh(c * (x + np.float32(0.044715) * (x * x * x)))))


def _resident(shape):
    zeros = (0,) * len(shape)
    return pl.BlockSpec(shape, lambda *_: zeros, pipeline_mode=pl.Buffered(1))


def _in_proj_body(x_ref, g1_ref, win_ref, qg_ref, kg_ref, cos_ref, sin_ref, lng_ref, lnb_ref,
                  q_ref, k_ref, v_ref, u_ref, vn_ref, *, attn_w, kv_w, gmlp_w):
    h = _rms(x_ref[...], g1_ref[...]).astype(_BF16)
    cosf = cos_ref[...]
    sinf = sin_ref[...]

    def qk_heads(col0, n_heads, gain, out_ref):
        y = _dot(h, win_ref[:, col0:col0 + n_heads * HEAD_DIM])
        for hd in range(n_heads):
            sl = slice(hd * HEAD_DIM, (hd + 1) * HEAD_DIM)
            t = _rms(y[:, sl], gain)
            out_ref[:, sl] = (t * cosf + pltpu.roll(t, HEAD_DIM // 2, 1) * sinf).astype(_BF16)

    c_v = attn_w + kv_w
    c_u = c_v + kv_w
    c_gv = c_u + gmlp_w
    qk_heads(0, attn_w // HEAD_DIM, qg_ref[...], q_ref)
    qk_heads(attn_w, kv_w // HEAD_DIM, kg_ref[...], k_ref)
    v_ref[...] = _dot(h, win_ref[:, c_v:c_v + kv_w]).astype(_BF16)
    u_ref[...] = _gelu_tanh(_dot(h, win_ref[:, c_u:c_u + gmlp_w])).astype(_BF16)
    gv = _gelu_tanh(_dot(h, win_ref[:, c_gv:c_gv + gmlp_w]))
    xc = gv - jnp.mean(gv, axis=-1, keepdims=True)
    vn = xc * lax.rsqrt(jnp.mean(xc * xc, axis=-1, keepdims=True) + EPS) * lng_ref[...] + lnb_ref[...]
    vn_ref[...] = vn.astype(_BF16)


def _in_proj(x2, g1, win, qg, kg, cosf, sinf, lng, lnb, *, seq, attn_w, kv_w, gmlp_w):
    tokens, d_model = x2.shape
    tm = IN_TM
    tiles_per_seq = seq // tm
    row = lambda w: pl.BlockSpec((tm, w), lambda m: (m, 0))
    pos = pl.BlockSpec((tm, HEAD_DIM), lambda m: (m % tiles_per_seq, 0))
    return pl.pallas_call(
        functools.partial(_in_proj_body, attn_w=attn_w, kv_w=kv_w, gmlp_w=gmlp_w),
        grid=(tokens // tm,),
        in_specs=[row(d_model), _resident(g1.shape), _resident(win.shape), _resident(qg.shape),
                  _resident(kg.shape), pos, pos, _resident(lng.shape), _resident(lnb.shape)],
        out_specs=[row(attn_w), row(kv_w), row(kv_w), row(gmlp_w), row(gmlp_w)],
        out_shape=[jax.ShapeDtypeStruct((tokens, w), _BF16) for w in (attn_w, kv_w, kv_w, gmlp_w, gmlp_w)],
        compiler_params=pltpu.CompilerParams(dimension_semantics=("parallel",),
                                             vmem_limit_bytes=VMEM_LIMIT_BYTES),
        name="in_proj",
    )(x2, g1, win, qg, kg, cosf, sinf, lng, lnb)


def _mixer_body(sink_ref, q_ref, k_ref, v_ref, u_ref, vn_ref, x_ref, wo_ref, ws_ref, bst_ref,
                ag_ref, sg_ref, g2_ref, xo_ref, hn_ref, attn_scr, sgu_scr, mix_scr,
                *, seq, n_kv, n_gmlp):
    tm = q_ref.shape[0]
    attn_w = q_ref.shape[1]
    blocks = tm // BLOCK
    first_block = pl.program_id(1) * blocks
    scale = np.float32(HEAD_DIM ** -0.5)

    def out_proj(rows):
        mix_scr[rows, :attn_w] = _rms(attn_scr[rows, :], ag_ref[...]).astype(_BF16)
        mix_scr[rows, attn_w:] = _rms(sgu_scr[rows, :], sg_ref[...]).astype(_BF16)
        x_new = x_ref[rows, :] + _dot(mix_scr[rows, :], wo_ref[...])
        xo_ref[rows, :] = x_new
        hn_ref[rows, :] = _rms(x_new, g2_ref[...]).astype(_BF16)

    for i in range(blocks):
        rs = slice(i * BLOCK, (i + 1) * BLOCK)
        n = first_block + i
        ks = pl.multiple_of(jnp.clip((n - 1) * BLOCK, 0, seq - BAND), BLOCK)
        rel = (ks - n * BLOCK) + (lax.broadcasted_iota(jnp.int32, (BLOCK, BAND), 1)
                                  - lax.broadcasted_iota(jnp.int32, (BLOCK, BAND), 0))
        valid = jnp.abs(rel) <= WINDOW
        for g in range(n_kv):
            kv_sl = slice(g * HEAD_DIM, (g + 1) * HEAD_DIM)
            kb = k_ref[pl.ds(ks, BAND), kv_sl]
            vb = v_ref[pl.ds(ks, BAND), kv_sl]
            heads = range(g * GQA_GROUP, (g + 1) * GQA_GROUP)
            q4 = jnp.concatenate([q_ref[rs, hd * HEAD_DIM:(hd + 1) * HEAD_DIM] for hd in heads], axis=0)
            s4 = lax.dot_general(q4, kb, (((1,), (1,)), ((), ())), preferred_element_type=_F32)
            ps, denoms = [], []
            for j, hd in enumerate(heads):
                s = jnp.where(valid, s4[j * BLOCK:(j + 1) * BLOCK] * scale, MASK_VALUE)
                sk = sink_ref[hd]
                m = jnp.maximum(jnp.max(s, axis=-1, keepdims=True), sk)
                p = jnp.exp(s - m)
                denoms.append(jnp.sum(p, axis=-1, keepdims=True) + jnp.exp(sk - m))
                ps.append(p.astype(_BF16))
            o4 = _dot(jnp.concatenate(ps, axis=0), vb)
            for j, hd in enumerate(heads):
                attn_scr[rs, hd * HEAD_DIM:(hd + 1) * HEAD_DIM] = o4[j * BLOCK:(j + 1) * BLOCK] / denoms[j]
        for hd in range(n_gmlp):
            sl = slice(hd * HEAD_DIM, (hd + 1) * HEAD_DIM)
            f = _dot(ws_ref[hd], vn_ref[rs, sl]) + bst_ref[:, hd:hd + 1]
            sgu_scr[rs, sl] = u_ref[rs, sl].astype(_F32) * f
        if (i + 1) % MIX_GROUP_BLOCKS == 0:
            out_proj(slice((i + 1 - MIX_GROUP_BLOCKS) * BLOCK, (i + 1) * BLOCK))


def _mixer(sink, q, k, v, u, vn, x2, wo, ws, bst, ag, sg, g2, *, batch, seq):
    tokens, d_model = x2.shape
    attn_w, kv_w, gmlp_w = q.shape[1], k.shape[1], u.shape[1]
    tm = MIX_TM
    tiles = seq // tm
    row = lambda w: pl.BlockSpec((tm, w), lambda b, t: (b * tiles + t, 0))
    whole_seq = pl.BlockSpec((seq, kv_w), lambda b, t: (b, 0))
    body = functools.partial(_mixer_body, seq=seq, n_kv=kv_w // HEAD_DIM, n_gmlp=gmlp_w // HEAD_DIM)
    return pl.pallas_call(
        body,
        grid=(batch, tiles),
        in_specs=[pl.BlockSpec(memory_space=pltpu.SMEM), row(attn_w), whole_seq, whole_seq, row(gmlp_w),
                  row(gmlp_w), row(d_model), _resident(wo.shape), _resident(ws.shape), _resident(bst.shape),
                  _resident(ag.shape), _resident(sg.shape), _resident(g2.shape)],
        out_specs=[row(d_model), row(d_model)],
        out_shape=[jax.ShapeDtypeStruct((tokens, d_model), _F32),
                   jax.ShapeDtypeStruct((tokens, d_model), _BF16)],
        scratch_shapes=[pltpu.VMEM((tm, attn_w), _F32), pltpu.VMEM((tm, gmlp_w), _F32),
                        pltpu.VMEM((tm, d_model), _BF16)],
        compiler_params=pltpu.CompilerParams(dimension_semantics=("parallel", "parallel"),
                                             vmem_limit_bytes=VMEM_LIMIT_BYTES),
        name="mixer",
    )(sink, q, k, v, u, vn, x2, wo, ws, bst, ag, sg, g2)


def _ffn_body(prev_ref, main_ref, next_ref, x_ref, wg_ref, wu_ref, cwg_ref, cwu_ref, cbg_ref, cbu_ref,
              wd_ref, o_ref, win_scr, *, tiles_per_seq):
    tm = main_ref.shape[0]
    m = pl.program_id(0)
    j = pl.program_id(1)

    @pl.when(j == 0)
    def _():
        t = m % tiles_per_seq
        win_scr[:HALO] = jnp.where(t == 0, jnp.zeros_like(prev_ref[...]), prev_ref[...])
        win_scr[HALO:HALO + tm] = main_ref[...]
        win_scr[HALO + tm:] = jnp.where(t == tiles_per_seq - 1, jnp.zeros_like(next_ref[...]), next_ref[...])
        o_ref[...] = x_ref[...]

    win = win_scr[...]
    wrows = tm + 2 * HALO

    def conv(w_ref, cw_ref, cb_ref):
        a = _dot(win, w_ref[...])
        below = pltpu.roll(a, 1, 0)
        above = pltpu.roll(a, wrows - 1, 0)
        c = cb_ref[...] + below * cw_ref[0:1, :] + a * cw_ref[1:2, :] + above * cw_ref[2:3, :]
        return c[HALO:HALO + tm]

    gate = conv(wg_ref, cwg_ref, cbg_ref)
    up = conv(wu_ref, cwu_ref, cbu_ref)
    act = (gate * (1.0 / (1.0 + jnp.exp(-gate))) * up).astype(_BF16)
    o_ref[...] += _dot(act, wd_ref[...])


def _ffn(hn, x2, w_up, conv_w, conv_b, w_down, *, seq):
    tokens, d_model = x2.shape
    d_ff = w_down.shape[0]
    tm, tf = FFN_TM, FFN_TF
    tiles_per_seq = seq // tm
    halo_blocks = tm // HALO
    last_halo = tokens // HALO - 1
    u_off = d_ff // tf
    row = lambda: pl.BlockSpec((tm, d_model), lambda m, j: (m, 0))
    return pl.pallas_call(
        functools.partial(_ffn_body, tiles_per_seq=tiles_per_seq),
        grid=(tokens // tm, d_ff // tf),
        in_specs=[
            pl.BlockSpec((HALO, d_model), lambda m, j: (jnp.maximum(m * halo_blocks - 1, 0), 0)),
            row(),
            pl.BlockSpec((HALO, d_model), lambda m, j: (jnp.minimum((m + 1) * halo_blocks, last_halo), 0)),
            pl.BlockSpec((tm, d_model), lambda m, j: (m, 0), pipeline_mode=pl.Buffered(1)),
            pl.BlockSpec((d_model, tf), lambda m, j: (0, j)),
            pl.BlockSpec((d_model, tf), lambda m, j: (0, j + u_off)),
            pl.BlockSpec((CONV_WIDTH, tf), lambda m, j: (0, j)),
            pl.BlockSpec((CONV_WIDTH, tf), lambda m, j: (0, j + u_off)),
            pl.BlockSpec((1, tf), lambda m, j: (0, j)),
            pl.BlockSpec((1, tf), lambda m, j: (0, j + u_off)),
            pl.BlockSpec((tf, d_model), lambda m, j: (j, 0)),
        ],
        out_specs=row(),
        out_shape=jax.ShapeDtypeStruct((tokens, d_model), _F32),
        scratch_shapes=[pltpu.VMEM((tm + 2 * HALO, d_model), _BF16)],
        compiler_params=pltpu.CompilerParams(dimension_semantics=("parallel", "arbitrary"),
                                             vmem_limit_bytes=VMEM_LIMIT_BYTES),
        name="ffn",
    )(hn, hn, hn, x2, w_up, w_up, conv_w, conv_w, conv_b, conv_b, w_down)


def _rope_tables(seq):
    inv_freq = ROPE_THETA ** (-jnp.arange(0, HEAD_DIM, 2, dtype=_F32) / HEAD_DIM)
    ang = jnp.arange(seq, dtype=_F32)[:, None] * inv_freq[None, :]
    cos, sin = jnp.cos(ang), jnp.sin(ang)
    return jnp.concatenate([cos, cos], axis=-1), jnp.concatenate([-sin, sin], axis=-1)


def kernel(x, norm1_g, w_in, q_norm_g, k_norm_g, sink, sgu_ln_g, sgu_ln_b, w_s, b_s,
           attn_out_g, sgu_out_g, w_o, norm2_g, w_up, conv_w, conv_b, w_down):
    batch, seq, d_model = x.shape
    depth = w_in.shape[0]
    attn_w = attn_out_g.shape[1]
    gmlp_w = sgu_out_g.shape[1]
    kv_w = (w_in.shape[2] - attn_w - 2 * gmlp_w) // 2
    assert seq % max(IN_TM, MIX_TM, FFN_TM) == 0 and w_down.shape[1] % FFN_TF == 0
    assert w_s.shape[2] == BLOCK and seq >= BAND

    cosf, sinf = _rope_tables(seq)
    row = lambda a: a.reshape(1, -1)
    x2 = x.reshape(batch * seq, d_model)
    for l in range(depth):
        q, k, v, u, vn = _in_proj(
            x2, row(norm1_g[l]), w_in[l].astype(_BF16), row(q_norm_g[l]), row(k_norm_g[l]), cosf, sinf,
            row(sgu_ln_g[l]), row(sgu_ln_b[l]), seq=seq, attn_w=attn_w, kv_w=kv_w, gmlp_w=gmlp_w)
        x2, hn = _mixer(
            sink[l], q, k, v, u, vn, x2, w_o[l].astype(_BF16), w_s[l].astype(_BF16), b_s[l].T,
            row(attn_out_g[l]), row(sgu_out_g[l]), row(norm2_g[l]), batch=batch, seq=seq)
        x2 = _ffn(hn, x2, w_up[l].astype(_BF16), conv_w[l], row(conv_b[l]), w_down[l].astype(_BF16), seq=seq)
    return x2.reshape(batch, seq, d_model)
```

```python
import functools

import jax
import jax.numpy as jnp
import numpy as np
from jax import lax
from jax.experimental import pallas as pl
from jax.experimental.pallas import tpu as pltpu

HEAD_DIM = 128
GQA_GROUP = 4
WINDOW = 128
BLOCK = 128
BAND = 3 * BLOCK
CONV_WIDTH = 3
ROPE_THETA = 10000.0
EPS = 1e-6
MASK_VALUE = -1e30

HALO = 16
VMEM_LIMIT_BYTES = 56 * 1024 * 1024

IN_TM = 512
MIX_TM = 512
MIX_GROUP_BLOCKS = 2
FFN_TM = 1024
FFN_TF = 512

_BF16 = jnp.bfloat16
_F32 = jnp.float32


def _dot(a, b):
    return jnp.dot(a, b, preferred_element_type=_F32)


def _rms(x, g):
    return x * lax.rsqrt(jnp.mean(x * x, axis=-1, keepdims=True) + EPS) * g


def _gelu_tanh(x):
    c = np.float32(np.sqrt(2.0 / np.pi))
    return x * (0.5 * (1.0 + jnp.tanh(c * (x + np.float32(0.044715) * (x * x * x)))))


def _layer_block(arr, layer):
    tail = (0,) * (arr.ndim - 1)
    return pl.BlockSpec((None,) + arr.shape[1:], lambda *_: (layer,) + tail, pipeline_mode=pl.Buffered(1))


def _in_proj_body(x_ref, g1_ref, win_ref, qg_ref, kg_ref, cos_ref, sin_ref, lng_ref, lnb_ref,
                  q_ref, k_ref, v_ref, u_ref, vn_ref, *, attn_w, kv_w, gmlp_w):
    h = _rms(x_ref[...], g1_ref[...]).astype(_BF16)
    cosf = cos_ref[...]
    sinf = sin_ref[...]

    def qk_heads(col0, n_heads, gain, out_ref):
        y = _dot(h, win_ref[:, col0:col0 + n_heads * HEAD_DIM])
        for hd in range(n_heads):
            sl = slice(hd * HEAD_DIM, (hd + 1) * HEAD_DIM)
            t = _rms(y[:, sl], gain)
            out_ref[:, sl] = (t * cosf + pltpu.roll(t, HEAD_DIM // 2, 1) * sinf).astype(_BF16)

    c_v = attn_w + kv_w
    c_u = c_v + kv_w
    c_gv = c_u + gmlp_w
    qk_heads(0, attn_w // HEAD_DIM, qg_ref[...], q_ref)
    qk_heads(attn_w, kv_w // HEAD_DIM, kg_ref[...], k_ref)
    v_ref[...] = _dot(h, win_ref[:, c_v:c_v + kv_w]).astype(_BF16)
    u_ref[...] = _gelu_tanh(_dot(h, win_ref[:, c_u:c_u + gmlp_w])).astype(_BF16)
    gv = _gelu_tanh(_dot(h, win_ref[:, c_gv:c_gv + gmlp_w]))
    xc = gv - jnp.mean(gv, axis=-1, keepdims=True)
    vn = xc * lax.rsqrt(jnp.mean(xc * xc, axis=-1, keepdims=True) + EPS) * lng_ref[...] + lnb_ref[...]
    vn_ref[...] = vn.astype(_BF16)


def _in_proj(layer, x2, g1, win, qg, kg, cosf, sinf, lng, lnb, *, seq, attn_w, kv_w, gmlp_w):
    tokens, d_model = x2.shape
    tm = IN_TM
    tiles_per_seq = seq // tm
    row = lambda w: pl.BlockSpec((tm, w), lambda m: (m, 0))
    pos = pl.BlockSpec((tm, HEAD_DIM), lambda m: (m % tiles_per_seq, 0))
    lb = functools.partial(_layer_block, layer=layer)
    return pl.pallas_call(
        functools.partial(_in_proj_body, attn_w=attn_w, kv_w=kv_w, gmlp_w=gmlp_w),
        grid=(tokens // tm,),
        in_specs=[row(d_model), lb(g1), lb(win), lb(qg), lb(kg), pos, pos, lb(lng), lb(lnb)],
        out_specs=[row(attn_w), row(kv_w), row(kv_w), row(gmlp_w), row(gmlp_w)],
        out_shape=[jax.ShapeDtypeStruct((tokens, w), _BF16) for w in (attn_w, kv_w, kv_w, gmlp_w, gmlp_w)],
        compiler_params=pltpu.CompilerParams(dimension_semantics=("parallel",),
                                             vmem_limit_bytes=VMEM_LIMIT_BYTES),
        name="in_proj",
    )(x2, g1, win, qg, kg, cosf, sinf, lng, lnb)


def _mixer_body(sink_ref, q_ref, k_ref, v_ref, u_ref, vn_ref, x_ref, wo_ref, ws_ref, bst_ref,
                ag_ref, sg_ref, g2_ref, xo_ref, hn_ref, attn_scr, sgu_scr, mix_scr,
                *, layer, seq, n_kv, n_gmlp):
    tm = q_ref.shape[0]
    attn_w = q_ref.shape[1]
    blocks = tm // BLOCK
    first_block = pl.program_id(1) * blocks
    scale = np.float32(HEAD_DIM ** -0.5)

    for i in range(blocks):
        rs = slice(i * BLOCK, (i + 1) * BLOCK)
        n = first_block + i
        ks = pl.multiple_of(jnp.clip((n - 1) * BLOCK, 0, seq - BAND), BLOCK)
        rel = (ks - n * BLOCK) + (lax.broadcasted_iota(jnp.int32, (BLOCK, BAND), 1)
                                  - lax.broadcasted_iota(jnp.int32, (BLOCK, BAND), 0))
        valid = jnp.abs(rel) <= WINDOW
        for g in range(n_kv):
            kv_sl = slice(g * HEAD_DIM, (g + 1) * HEAD_DIM)
            kb = k_ref[pl.ds(ks, BAND), kv_sl]
            vb = v_ref[pl.ds(ks, BAND), kv_sl]
            heads = range(g * GQA_GROUP, (g + 1) * GQA_GROUP)
            q4 = jnp.concatenate([q_ref[rs, hd * HEAD_DIM:(hd + 1) * HEAD_DIM] for hd in heads], axis=0)
            s4 = lax.dot_general(q4, kb, (((1,), (1,)), ((), ())), preferred_element_type=_F32)
            ps, denoms = [], []
            for j, hd in enumerate(heads):
                s = jnp.where(valid, s4[j * BLOCK:(j + 1) * BLOCK] * scale, MASK_VALUE)
                sk = sink_ref[layer, hd]
                m = jnp.maximum(jnp.max(s, axis=-1, keepdims=True), sk)
                p = jnp.exp(s - m)
                denoms.append(jnp.sum(p, axis=-1, keepdims=True) + jnp.exp(sk - m))
                ps.append(p.astype(_BF16))
            o4 = _dot(jnp.concatenate(ps, axis=0), vb)
            for j, hd in enumerate(heads):
                attn_scr[rs, hd * HEAD_DIM:(hd + 1) * HEAD_DIM] = o4[j * BLOCK:(j + 1) * BLOCK] / denoms[j]
        for hd in range(n_gmlp):
            sl = slice(hd * HEAD_DIM, (hd + 1) * HEAD_DIM)
            f = _dot(ws_ref[hd], vn_ref[rs, sl]) + bst_ref[:, hd:hd + 1]
            sgu_scr[rs, sl] = u_ref[rs, sl].astype(_F32) * f

    group = MIX_GROUP_BLOCKS * BLOCK
    for r0 in range(0, tm, group):
        rows = slice(r0, r0 + group)
        mix_scr[rows, :attn_w] = _rms(attn_scr[rows, :], ag_ref[...]).astype(_BF16)
        mix_scr[rows, attn_w:] = _rms(sgu_scr[rows, :], sg_ref[...]).astype(_BF16)
        x_new = x_ref[rows, :] + _dot(mix_scr[rows, :], wo_ref[...])
        xo_ref[rows, :] = x_new
        hn_ref[rows, :] = _rms(x_new, g2_ref[...]).astype(_BF16)


def _mixer(layer, sink, q, k, v, u, vn, x2, wo, ws, bst, ag, sg, g2, *, batch, seq):
    tokens, d_model = x2.shape
    attn_w, kv_w, gmlp_w = q.shape[1], k.shape[1], u.shape[1]
    tm = MIX_TM
    tiles = seq // tm
    row = lambda w: pl.BlockSpec((tm, w), lambda b, t: (b * tiles + t, 0))
    whole_seq = pl.BlockSpec((seq, kv_w), lambda b, t: (b, 0))
    lb = functools.partial(_layer_block, layer=layer)
    body = functools.partial(_mixer_body, layer=layer, seq=seq, n_kv=kv_w // HEAD_DIM,
                             n_gmlp=gmlp_w // HEAD_DIM)
    return pl.pallas_call(
        body,
        grid=(batch, tiles),
        in_specs=[pl.BlockSpec(memory_space=pltpu.SMEM), row(attn_w), whole_seq, whole_seq, row(gmlp_w),
                  row(gmlp_w), row(d_model), lb(wo), lb(ws), lb(bst), lb(ag), lb(sg), lb(g2)],
        out_specs=[row(d_model), row(d_model)],
        out_shape=[jax.ShapeDtypeStruct((tokens, d_model), _F32),
                   jax.ShapeDtypeStruct((tokens, d_model), _BF16)],
        scratch_shapes=[pltpu.VMEM((tm, attn_w), _F32), pltpu.VMEM((tm, gmlp_w), _F32),
                        pltpu.VMEM((tm, d_model), _BF16)],
        compiler_params=pltpu.CompilerParams(dimension_semantics=("parallel", "parallel"),
                                             vmem_limit_bytes=VMEM_LIMIT_BYTES),
        name="mixer",
    )(sink, q, k, v, u, vn, x2, wo, ws, bst, ag, sg, g2)


def _ffn_body(prev_ref, main_ref, next_ref, x_ref, wg_ref, wu_ref, cwg_ref, cwu_ref, cbg_ref, cbu_ref,
              wd_ref, o_ref, win_scr, *, tiles_per_seq):
    tm = main_ref.shape[0]
    m = pl.program_id(0)
    j = pl.program_id(1)

    @pl.when(j == 0)
    def _():
        t = m % tiles_per_seq
        win_scr[:HALO] = jnp.where(t == 0, jnp.zeros_like(prev_ref[...]), prev_ref[...])
        win_scr[HALO:HALO + tm] = main_ref[...]
        win_scr[HALO + tm:] = jnp.where(t == tiles_per_seq - 1, jnp.zeros_like(next_ref[...]), next_ref[...])
        o_ref[...] = x_ref[...]

    win = win_scr[...]
    wrows = tm + 2 * HALO

    def conv(w_ref, cw_ref, cb_ref):
        a = _dot(win, w_ref[...])
        below = pltpu.roll(a, 1, 0)
        above = pltpu.roll(a, wrows - 1, 0)
        c = cb_ref[...] + below * cw_ref[0:1, :] + a * cw_ref[1:2, :] + above * cw_ref[2:3, :]
        return c[HALO:HALO + tm]

    gate = conv(wg_ref, cwg_ref, cbg_ref)
    up = conv(wu_ref, cwu_ref, cbu_ref)
    act = (gate * (1.0 / (1.0 + jnp.exp(-gate))) * up).astype(_BF16)
    o_ref[...] += _dot(act, wd_ref[...])


def _ffn(layer, hn, x2, w_up, conv_w, conv_b, w_down, *, seq):
    tokens, d_model = x2.shape
    d_ff = w_down.shape[1]
    tm, tf = FFN_TM, FFN_TF
    tiles_per_seq = seq // tm
    halo_blocks = tm // HALO
    last_halo = tokens // HALO - 1
    u_off = d_ff // tf
    row = lambda: pl.BlockSpec((tm, d_model), lambda m, j: (m, 0))
    return pl.pallas_call(
        functools.partial(_ffn_body, tiles_per_seq=tiles_per_seq),
        grid=(tokens // tm, d_ff // tf),
        in_specs=[
            pl.BlockSpec((HALO, d_model), lambda m, j: (jnp.maximum(m * halo_blocks - 1, 0), 0)),
            row(),
            pl.BlockSpec((HALO, d_model), lambda m, j: (jnp.minimum((m + 1) * halo_blocks, last_halo), 0)),
            pl.BlockSpec((tm, d_model), lambda m, j: (m, 0), pipeline_mode=pl.Buffered(1)),
            pl.BlockSpec((None, d_model, tf), lambda m, j: (layer, 0, j)),
            pl.BlockSpec((None, d_model, tf), lambda m, j: (layer, 0, j + u_off)),
            pl.BlockSpec((None, CONV_WIDTH, tf), lambda m, j: (layer, 0, j)),
            pl.BlockSpec((None, CONV_WIDTH, tf), lambda m, j: (layer, 0, j + u_off)),
            pl.BlockSpec((None, 1, tf), lambda m, j: (layer, 0, j)),
            pl.BlockSpec((None, 1, tf), lambda m, j: (layer, 0, j + u_off)),
            pl.BlockSpec((None, tf, d_model), lambda m, j: (layer, j, 0)),
        ],
        out_specs=row(),
        out_shape=jax.ShapeDtypeStruct((tokens, d_model), _F32),
        scratch_shapes=[pltpu.VMEM((tm + 2 * HALO, d_model), _BF16)],
        compiler_params=pltpu.CompilerParams(dimension_semantics=("parallel", "arbitrary"),
                                             vmem_limit_bytes=VMEM_LIMIT_BYTES),
        name="ffn",
    )(hn, hn, hn, x2, w_up, w_up, conv_w, conv_w, conv_b, conv_b, w_down)


def _rope_tables(seq):
    inv_freq = ROPE_THETA ** (-jnp.arange(0, HEAD_DIM, 2, dtype=_F32) / HEAD_DIM)
    ang = jnp.arange(seq, dtype=_F32)[:, None] * inv_freq[None, :]
    cos, sin = jnp.cos(ang), jnp.sin(ang)
    return jnp.concatenate([cos, cos], axis=-1), jnp.concatenate([-sin, sin], axis=-1)


def kernel(x, norm1_g, w_in, q_norm_g, k_norm_g, sink, sgu_ln_g, sgu_ln_b, w_s, b_s,
           attn_out_g, sgu_out_g, w_o, norm2_g, w_up, conv_w, conv_b, w_down):
    batch, seq, d_model = x.shape
    depth = w_in.shape[0]
    attn_w = attn_out_g.shape[1]
    gmlp_w = sgu_out_g.shape[1]
    kv_w = (w_in.shape[2] - attn_w - 2 * gmlp_w) // 2
    assert seq % max(IN_TM, MIX_TM, FFN_TM) == 0 and w_down.shape[1] % FFN_TF == 0
    assert w_s.shape[2] == BLOCK and seq >= BAND and MIX_TM % (MIX_GROUP_BLOCKS * BLOCK) == 0

    cosf, sinf = _rope_tables(seq)
    rows = lambda a: a.reshape(depth, 1, -1)
    g1, qg, kg, lng, lnb = (rows(a) for a in (norm1_g, q_norm_g, k_norm_g, sgu_ln_g, sgu_ln_b))
    ag, sg, g2, cb = (rows(a) for a in (attn_out_g, sgu_out_g, norm2_g, conv_b))
    bst = jnp.swapaxes(b_s, 1, 2)
    win, wo, ws, wup, wdn = (w.astype(_BF16) for w in (w_in, w_o, w_s, w_up, w_down))

    x2 = x.reshape(batch * seq, d_model)
    for l in range(depth):
        q, k, v, u, vn = _in_proj(l, x2, g1, win, qg, kg, cosf, sinf, lng, lnb,
                                  seq=seq, attn_w=attn_w, kv_w=kv_w, gmlp_w=gmlp_w)
        x2, hn = _mixer(l, sink, q, k, v, u, vn, x2, wo, ws, bst, ag, sg, g2, batch=batch, seq=seq)
        x2 = _ffn(l, hn, x2, wup, conv_w, cb, wdn, seq=seq)
    return x2.reshape(batch, seq, d_model)
```

```python
import functools

import jax
import jax.numpy as jnp
import numpy as np
from jax import lax
from jax.experimental import pallas as pl
from jax.experimental.pallas import tpu as pltpu

HEAD_DIM = 128
GQA_GROUP = 4
WINDOW = 128
BLOCK = 128
BAND = 3 * BLOCK
CONV_WIDTH = 3
ROPE_THETA = 10000.0
EPS = 1e-6
MASK_VALUE = -1e30

BF16_TILE_ROWS = 16
HALO = BF16_TILE_ROWS
VMEM_LIMIT_BYTES = 56 * 1024 * 1024

IN_TM = 512
MIX_TM = 512
MIX_GROUP_BLOCKS = 2
FFN_TM = 1024
FFN_TF = 512

_BF16 = jnp.bfloat16
_F32 = jnp.float32


def _dot(a, b):
    return jnp.dot(a, b, preferred_element_type=_F32)


def _rms(x, g):
    return x * lax.rsqrt(jnp.mean(x * x, axis=-1, keepdims=True) + EPS) * g


def _gelu_tanh(x):
    c = np.float32(np.sqrt(2.0 / np.pi))
    return x * (0.5 * (1.0 + jnp.tanh(c * (x + np.float32(0.044715) * (x * x * x)))))


def _layer_block(arr, layer):
    tail = (0,) * (arr.ndim - 1)
    return pl.BlockSpec((None,) + arr.shape[1:], lambda *_: (layer,) + tail, pipeline_mode=pl.Buffered(1))


def _in_proj_body(x_ref, g1_ref, win_ref, qg_ref, kg_ref, cos_ref, sin_ref, lng_ref, lnb_ref,
                  wup_ref, wdn_ref, q_ref, k_ref, v_ref, u_ref, vn_ref, wup_out_ref, wdn_out_ref,
                  *, attn_w, kv_w, gmlp_w):
    wup_out_ref[...] = wup_ref[...].astype(_BF16)
    wdn_out_ref[...] = wdn_ref[...].astype(_BF16)
    h = _rms(x_ref[...], g1_ref[...]).astype(_BF16)
    cosf = cos_ref[...]
    sinf = sin_ref[...]

    def qk_heads(col0, n_heads, gain, out_ref):
        y = _dot(h, win_ref[:, col0:col0 + n_heads * HEAD_DIM])
        for hd in range(n_heads):
            sl = slice(hd * HEAD_DIM, (hd + 1) * HEAD_DIM)
            t = _rms(y[:, sl], gain)
            out_ref[:, sl] = (t * cosf + pltpu.roll(t, HEAD_DIM // 2, 1) * sinf).astype(_BF16)

    c_v = attn_w + kv_w
    c_u = c_v + kv_w
    c_gv = c_u + gmlp_w
    qk_heads(0, attn_w // HEAD_DIM, qg_ref[...], q_ref)
    qk_heads(attn_w, kv_w // HEAD_DIM, kg_ref[...], k_ref)
    v_ref[...] = _dot(h, win_ref[:, c_v:c_v + kv_w]).astype(_BF16)
    u_ref[...] = _gelu_tanh(_dot(h, win_ref[:, c_u:c_u + gmlp_w])).astype(_BF16)
    gv = _gelu_tanh(_dot(h, win_ref[:, c_gv:c_gv + gmlp_w]))
    xc = gv - jnp.mean(gv, axis=-1, keepdims=True)
    vn = xc * lax.rsqrt(jnp.mean(xc * xc, axis=-1, keepdims=True) + EPS) * lng_ref[...] + lnb_ref[...]
    vn_ref[...] = vn.astype(_BF16)


def _in_proj(layer, x2, g1, win, qg, kg, cosf, sinf, lng, lnb, w_up, w_down, *, seq, attn_w, kv_w, gmlp_w):
    tokens, d_model = x2.shape
    tm = IN_TM
    steps = tokens // tm
    tiles_per_seq = seq // tm
    row = lambda w: pl.BlockSpec((tm, w), lambda m: (m, 0))
    pos = pl.BlockSpec((tm, HEAD_DIM), lambda m: (m % tiles_per_seq, 0))
    lb = functools.partial(_layer_block, layer=layer)
    assert w_up.shape[1] % (steps * BF16_TILE_ROWS) == 0 and w_down.shape[1] % (steps * BF16_TILE_ROWS) == 0
    up_rows, up_cols = w_up.shape[1] // steps, w_up.shape[2]
    dn_rows, dn_cols = w_down.shape[1] // steps, w_down.shape[2]
    return pl.pallas_call(
        functools.partial(_in_proj_body, attn_w=attn_w, kv_w=kv_w, gmlp_w=gmlp_w),
        grid=(steps,),
        in_specs=[row(d_model), lb(g1), lb(win), lb(qg), lb(kg), pos, pos, lb(lng), lb(lnb),
                  pl.BlockSpec((None, up_rows, up_cols), lambda m: (layer, m, 0)),
                  pl.BlockSpec((None, dn_rows, dn_cols), lambda m: (layer, m, 0))],
        out_specs=[row(attn_w), row(kv_w), row(kv_w), row(gmlp_w), row(gmlp_w),
                   pl.BlockSpec((up_rows, up_cols), lambda m: (m, 0)),
                   pl.BlockSpec((dn_rows, dn_cols), lambda m: (m, 0))],
        out_shape=[jax.ShapeDtypeStruct((tokens, w), _BF16) for w in (attn_w, kv_w, kv_w, gmlp_w, gmlp_w)]
        + [jax.ShapeDtypeStruct(w_up.shape[1:], _BF16), jax.ShapeDtypeStruct(w_down.shape[1:], _BF16)],
        compiler_params=pltpu.CompilerParams(dimension_semantics=("parallel",),
                                             vmem_limit_bytes=VMEM_LIMIT_BYTES),
        name="in_proj",
    )(x2, g1, win, qg, kg, cosf, sinf, lng, lnb, w_up, w_down)


def _mixer_body(sink_ref, q_ref, k_ref, v_ref, u_ref, vn_ref, x_ref, wo_ref, ws_ref, bst_ref,
                ag_ref, sg_ref, g2_ref, xo_ref, hn_ref, attn_scr, sgu_scr, mix_scr,
                *, layer, seq, n_kv, n_gmlp):
    tm = q_ref.shape[0]
    attn_w = q_ref.shape[1]
    blocks = tm // BLOCK
    first_block = pl.program_id(1) * blocks
    scale = np.float32(HEAD_DIM ** -0.5)

    for i in range(blocks):
        rs = slice(i * BLOCK, (i + 1) * BLOCK)
        n = first_block + i
        ks = pl.multiple_of(jnp.clip((n - 1) * BLOCK, 0, seq - BAND), BLOCK)
        rel = (ks - n * BLOCK) + (lax.broadcasted_iota(jnp.int32, (BLOCK, BAND), 1)
                                  - lax.broadcasted_iota(jnp.int32, (BLOCK, BAND), 0))
        valid = jnp.abs(rel) <= WINDOW
        for g in range(n_kv):
            kv_sl = slice(g * HEAD_DIM, (g + 1) * HEAD_DIM)
            kb = k_ref[pl.ds(ks, BAND), kv_sl]
            vb = v_ref[pl.ds(ks, BAND), kv_sl]
            heads = range(g * GQA_GROUP, (g + 1) * GQA_GROUP)
            q4 = jnp.concatenate([q_ref[rs, hd * HEAD_DIM:(hd + 1) * HEAD_DIM] for hd in heads], axis=0)
            s4 = lax.dot_general(q4, kb, (((1,), (1,)), ((), ())), preferred_element_type=_F32)
            ps, denoms = [], []
            for j, hd in enumerate(heads):
                s = jnp.where(valid, s4[j * BLOCK:(j + 1) * BLOCK] * scale, MASK_VALUE)
                sk = sink_ref[layer, hd]
                m = jnp.maximum(jnp.max(s, axis=-1, keepdims=True), sk)
                p = jnp.exp(s - m)
                denoms.append(jnp.sum(p, axis=-1, keepdims=True) + jnp.exp(sk - m))
                ps.append(p.astype(_BF16))
            o4 = _dot(jnp.concatenate(ps, axis=0), vb)
            for j, hd in enumerate(heads):
                attn_scr[rs, hd * HEAD_DIM:(hd + 1) * HEAD_DIM] = o4[j * BLOCK:(j + 1) * BLOCK] / denoms[j]
        for hd in range(n_gmlp):
            sl = slice(hd * HEAD_DIM, (hd + 1) * HEAD_DIM)
            f = _dot(ws_ref[hd], vn_ref[rs, sl]) + bst_ref[:, hd:hd + 1]
            sgu_scr[rs, sl] = u_ref[rs, sl].astype(_F32) * f

    group = MIX_GROUP_BLOCKS * BLOCK
    for r0 in range(0, tm, group):
        rows = slice(r0, r0 + group)
        mix_scr[rows, :attn_w] = _rms(attn_scr[rows, :], ag_ref[...]).astype(_BF16)
        mix_scr[rows, attn_w:] = _rms(sgu_scr[rows, :], sg_ref[...]).astype(_BF16)
        x_new = x_ref[rows, :] + _dot(mix_scr[rows, :], wo_ref[...])
        xo_ref[rows, :] = x_new
        hn_ref[rows, :] = _rms(x_new, g2_ref[...]).astype(_BF16)


def _mixer(layer, sink, q, k, v, u, vn, x2, wo, ws, bst, ag, sg, g2, *, batch, seq):
    tokens, d_model = x2.shape
    attn_w, kv_w, gmlp_w = q.shape[1], k.shape[1], u.shape[1]
    tm = MIX_TM
    tiles = seq // tm
    row = lambda w: pl.BlockSpec((tm, w), lambda b, t: (b * tiles + t, 0))
    whole_seq = pl.BlockSpec((seq, kv_w), lambda b, t: (b, 0))
    lb = functools.partial(_layer_block, layer=layer)
    body = functools.partial(_mixer_body, layer=layer, seq=seq, n_kv=kv_w // HEAD_DIM,
                             n_gmlp=gmlp_w // HEAD_DIM)
    return pl.pallas_call(
        body,
        grid=(batch, tiles),
        in_specs=[pl.BlockSpec(memory_space=pltpu.SMEM), row(attn_w), whole_seq, whole_seq, row(gmlp_w),
                  row(gmlp_w), row(d_model), lb(wo), lb(ws), lb(bst), lb(ag), lb(sg), lb(g2)],
        out_specs=[row(d_model), row(d_model)],
        out_shape=[jax.ShapeDtypeStruct((tokens, d_model), _F32),
                   jax.ShapeDtypeStruct((tokens, d_model), _BF16)],
        scratch_shapes=[pltpu.VMEM((tm, attn_w), _F32), pltpu.VMEM((tm, gmlp_w), _F32),
                        pltpu.VMEM((tm, d_model), _BF16)],
        compiler_params=pltpu.CompilerParams(dimension_semantics=("parallel", "parallel"),
                                             vmem_limit_bytes=VMEM_LIMIT_BYTES),
        name="mixer",
    )(sink, q, k, v, u, vn, x2, wo, ws, bst, ag, sg, g2)


def _ffn_body(prev_ref, main_ref, next_ref, x_ref, wg_ref, wu_ref, cwg_ref, cwu_ref, cbg_ref, cbu_ref,
              wd_ref, o_ref, win_scr, *, tiles_per_seq):
    tm = main_ref.shape[0]
    m = pl.program_id(0)
    j = pl.program_id(1)

    @pl.when(j == 0)
    def _():
        t = m % tiles_per_seq
        win_scr[:HALO] = jnp.where(t == 0, jnp.zeros_like(prev_ref[...]), prev_ref[...])
        win_scr[HALO:HALO + tm] = main_ref[...]
        win_scr[HALO + tm:] = jnp.where(t == tiles_per_seq - 1, jnp.zeros_like(next_ref[...]), next_ref[...])
        o_ref[...] = x_ref[...]

    win = win_scr[...]
    wrows = tm + 2 * HALO

    def conv(w_ref, cw_ref, cb_ref):
        a = _dot(win, w_ref[...])
        below = pltpu.roll(a, 1, 0)
        above = pltpu.roll(a, wrows - 1, 0)
        c = cb_ref[...] + below * cw_ref[0:1, :] + a * cw_ref[1:2, :] + above * cw_ref[2:3, :]
        return c[HALO:HALO + tm]

    gate = conv(wg_ref, cwg_ref, cbg_ref)
    up = conv(wu_ref, cwu_ref, cbu_ref)
    act = (gate * (1.0 / (1.0 + jnp.exp(-gate))) * up).astype(_BF16)
    o_ref[...] += _dot(act, wd_ref[...])


def _ffn(layer, hn, x2, w_up, conv_w, conv_b, w_down, *, seq):
    tokens, d_model = x2.shape
    d_ff = w_down.shape[0]
    tm, tf = FFN_TM, FFN_TF
    tiles_per_seq = seq // tm
    halo_blocks = tm // HALO
    last_halo = tokens // HALO - 1
    u_off = d_ff // tf
    row = lambda: pl.BlockSpec((tm, d_model), lambda m, j: (m, 0))
    return pl.pallas_call(
        functools.partial(_ffn_body, tiles_per_seq=tiles_per_seq),
        grid=(tokens // tm, d_ff // tf),
        in_specs=[
            pl.BlockSpec((HALO, d_model), lambda m, j: (jnp.maximum(m * halo_blocks - 1, 0), 0)),
            row(),
            pl.BlockSpec((HALO, d_model), lambda m, j: (jnp.minimum((m + 1) * halo_blocks, last_halo), 0)),
            pl.BlockSpec((tm, d_model), lambda m, j: (m, 0), pipeline_mode=pl.Buffered(1)),
            pl.BlockSpec((d_model, tf), lambda m, j: (0, j)),
            pl.BlockSpec((d_model, tf), lambda m, j: (0, j + u_off)),
            pl.BlockSpec((None, CONV_WIDTH, tf), lambda m, j: (layer, 0, j)),
            pl.BlockSpec((None, CONV_WIDTH, tf), lambda m, j: (layer, 0, j + u_off)),
            pl.BlockSpec((None, 1, tf), lambda m, j: (layer, 0, j)),
            pl.BlockSpec((None, 1, tf), lambda m, j: (layer, 0, j + u_off)),
            pl.BlockSpec((tf, d_model), lambda m, j: (j, 0)),
        ],
        out_specs=row(),
        out_shape=jax.ShapeDtypeStruct((tokens, d_model), _F32),
        scratch_shapes=[pltpu.VMEM((tm + 2 * HALO, d_model), _BF16)],
        compiler_params=pltpu.CompilerParams(dimension_semantics=("parallel", "arbitrary"),
                                             vmem_limit_bytes=VMEM_LIMIT_BYTES),
        name="ffn",
    )(hn, hn, hn, x2, w_up, w_up, conv_w, conv_w, conv_b, conv_b, w_down)


def _rope_tables(seq):
    inv_freq = ROPE_THETA ** (-jnp.arange(0, HEAD_DIM, 2, dtype=_F32) / HEAD_DIM)
    ang = jnp.arange(seq, dtype=_F32)[:, None] * inv_freq[None, :]
    cos, sin = jnp.cos(ang), jnp.sin(ang)
    return jnp.concatenate([cos, cos], axis=-1), jnp.concatenate([-sin, sin], axis=-1)


def kernel(x, norm1_g, w_in, q_norm_g, k_norm_g, sink, sgu_ln_g, sgu_ln_b, w_s, b_s,
           attn_out_g, sgu_out_g, w_o, norm2_g, w_up, conv_w, conv_b, w_down):
    batch, seq, d_model = x.shape
    depth = w_in.shape[0]
    attn_w = attn_out_g.shape[1]
    gmlp_w = sgu_out_g.shape[1]
    kv_w = (w_in.shape[2] - attn_w - 2 * gmlp_w) // 2
    assert seq % max(IN_TM, MIX_TM, FFN_TM) == 0 and w_down.shape[1] % FFN_TF == 0
    assert w_s.shape[2] == BLOCK and seq >= BAND and MIX_TM % (MIX_GROUP_BLOCKS * BLOCK) == 0

    cosf, sinf = _rope_tables(seq)
    rows = lambda a: a.reshape(depth, 1, -1)
    g1, qg, kg, lng, lnb = (rows(a) for a in (norm1_g, q_norm_g, k_norm_g, sgu_ln_g, sgu_ln_b))
    ag, sg, g2, cb = (rows(a) for a in (attn_out_g, sgu_out_g, norm2_g, conv_b))
    bst = jnp.swapaxes(b_s, 1, 2)
    win, wo, ws = (w.astype(_BF16) for w in (w_in, w_o, w_s))

    x2 = x.reshape(batch * seq, d_model)
    for l in range(depth):
        q, k, v, u, vn, wup, wdn = _in_proj(l, x2, g1, win, qg, kg, cosf, sinf, lng, lnb, w_up, w_down,
                                            seq=seq, attn_w=attn_w, kv_w=kv_w, gmlp_w=gmlp_w)
        x2, hn = _mixer(l, sink, q, k, v, u, vn, x2, wo, ws, bst, ag, sg, g2, batch=batch, seq=seq)
        x2 = _ffn(l, hn, x2, wup, conv_w, cb, wdn, seq=seq)
    return x2.reshape(batch, seq, d_model)
```

```python
import functools

import jax
import jax.numpy as jnp
import numpy as np
from jax import lax
from jax.experimental import pallas as pl
from jax.experimental.pallas import tpu as pltpu

HEAD_DIM = 128
GQA_GROUP = 4
WINDOW = 128
BLOCK = 128
BAND = 3 * BLOCK
CONV_WIDTH = 3
ROPE_THETA = 10000.0
EPS = 1e-6
MASK_VALUE = -1e30

BF16_TILE_ROWS = 16
LANES = 128
HALO = BF16_TILE_ROWS
VMEM_LIMIT_BYTES = 56 * 1024 * 1024

IN_TM = 512
MIX_TM = 512
MIX_GROUP_BLOCKS = 2
FFN_TM = 1024
FFN_TF = 512

_BF16 = jnp.bfloat16
_F32 = jnp.float32


def _dot(a, b):
    return jnp.dot(a, b, preferred_element_type=_F32)


def _rms(x, g):
    return x * lax.rsqrt(jnp.mean(x * x, axis=-1, keepdims=True) + EPS) * g


def _gelu_tanh(x):
    c = np.float32(np.sqrt(2.0 / np.pi))
    return x * (0.5 * (1.0 + jnp.tanh(c * (x + np.float32(0.044715) * (x * x * x)))))


def _layer_block(arr, layer):
    tail = (0,) * (arr.ndim - 1)
    return pl.BlockSpec((None,) + arr.shape[1:], lambda *_: (layer,) + tail, pipeline_mode=pl.Buffered(1))


def _in_proj_body(x_ref, g1_ref, win_ref, qg_ref, kg_ref, cos_ref, sin_ref, lng_ref, lnb_ref,
                  wup_ref, wdn_ref, q_ref, k_ref, v_ref, u_ref, vn_ref, wup_out_ref, wdn_out_ref,
                  *, attn_w, kv_w, gmlp_w):
    wup_out_ref[...] = wup_ref[...].astype(_BF16)
    wdn_out_ref[...] = wdn_ref[...].astype(_BF16)
    h = _rms(x_ref[...], g1_ref[...]).astype(_BF16)
    cosf = cos_ref[...]
    sinf = sin_ref[...]

    def qk_heads(col0, n_heads, gain, out_ref):
        y = _dot(h, win_ref[:, col0:col0 + n_heads * HEAD_DIM])
        for hd in range(n_heads):
            sl = slice(hd * HEAD_DIM, (hd + 1) * HEAD_DIM)
            t = _rms(y[:, sl], gain)
            out_ref[:, sl] = (t * cosf + pltpu.roll(t, HEAD_DIM // 2, 1) * sinf).astype(_BF16)

    c_v = attn_w + kv_w
    c_u = c_v + kv_w
    c_gv = c_u + gmlp_w
    qk_heads(0, attn_w // HEAD_DIM, qg_ref[...], q_ref)
    qk_heads(attn_w, kv_w // HEAD_DIM, kg_ref[...], k_ref)
    v_ref[...] = _dot(h, win_ref[:, c_v:c_v + kv_w]).astype(_BF16)
    u_ref[...] = _gelu_tanh(_dot(h, win_ref[:, c_u:c_u + gmlp_w])).astype(_BF16)
    gv = _gelu_tanh(_dot(h, win_ref[:, c_gv:c_gv + gmlp_w]))
    xc = gv - jnp.mean(gv, axis=-1, keepdims=True)
    vn = xc * lax.rsqrt(jnp.mean(xc * xc, axis=-1, keepdims=True) + EPS) * lng_ref[...] + lnb_ref[...]
    vn_ref[...] = vn.astype(_BF16)


def _in_proj(layer, x2, g1, win, qg, kg, cosf, sinf, lng, lnb, w_up, w_down, *, seq, attn_w, kv_w, gmlp_w):
    tokens, d_model = x2.shape
    tm = IN_TM
    steps = tokens // tm
    tiles_per_seq = seq // tm
    row = lambda w: pl.BlockSpec((tm, w), lambda m: (m, 0))
    pos = pl.BlockSpec((tm, HEAD_DIM), lambda m: (m % tiles_per_seq, 0))
    lb = functools.partial(_layer_block, layer=layer)
    assert w_up.shape[1] % (steps * BF16_TILE_ROWS) == 0 and w_down.shape[1] % (steps * BF16_TILE_ROWS) == 0
    up_rows, up_cols = w_up.shape[1] // steps, w_up.shape[2]
    dn_rows, dn_cols = w_down.shape[1] // steps, w_down.shape[2]
    return pl.pallas_call(
        functools.partial(_in_proj_body, attn_w=attn_w, kv_w=kv_w, gmlp_w=gmlp_w),
        grid=(steps,),
        in_specs=[row(d_model), lb(g1), lb(win), lb(qg), lb(kg), pos, pos, lb(lng), lb(lnb),
                  pl.BlockSpec((None, up_rows, up_cols), lambda m: (layer, m, 0)),
                  pl.BlockSpec((None, dn_rows, dn_cols), lambda m: (layer, m, 0))],
        out_specs=[row(attn_w), row(kv_w), row(kv_w), row(gmlp_w), row(gmlp_w),
                   pl.BlockSpec((up_rows, up_cols), lambda m: (m, 0)),
                   pl.BlockSpec((dn_rows, dn_cols), lambda m: (m, 0))],
        out_shape=[jax.ShapeDtypeStruct((tokens, w), _BF16) for w in (attn_w, kv_w, kv_w, gmlp_w, gmlp_w)]
        + [jax.ShapeDtypeStruct(w_up.shape[1:], _BF16), jax.ShapeDtypeStruct(w_down.shape[1:], _BF16)],
        compiler_params=pltpu.CompilerParams(dimension_semantics=("parallel",),
                                             vmem_limit_bytes=VMEM_LIMIT_BYTES),
        name="in_proj",
    )(x2, g1, win, qg, kg, cosf, sinf, lng, lnb, w_up, w_down)


def _mixer_body(sink_ref, q_ref, k_ref, v_ref, u_ref, vn_ref, x_ref, wo_ref, ws_ref, bst_ref,
                ag_ref, sg_ref, g2_ref, xo_ref, hn_ref, attn_scr, sgu_scr, mix_scr,
                *, layer, seq, n_kv, n_gmlp):
    tm = q_ref.shape[0]
    attn_w = q_ref.shape[1]
    blocks = tm // BLOCK
    first_block = pl.program_id(1) * blocks
    scale = np.float32(HEAD_DIM ** -0.5)

    for i in range(blocks):
        rs = slice(i * BLOCK, (i + 1) * BLOCK)
        n = first_block + i
        ks = pl.multiple_of(jnp.clip((n - 1) * BLOCK, 0, seq - BAND), BLOCK)
        rel = (ks - n * BLOCK) + (lax.broadcasted_iota(jnp.int32, (BLOCK, BAND), 1)
                                  - lax.broadcasted_iota(jnp.int32, (BLOCK, BAND), 0))
        valid = jnp.abs(rel) <= WINDOW
        for g in range(n_kv):
            kv_sl = slice(g * HEAD_DIM, (g + 1) * HEAD_DIM)
            kb = k_ref[pl.ds(ks, BAND), kv_sl]
            vb = v_ref[pl.ds(ks, BAND), kv_sl]
            heads = range(g * GQA_GROUP, (g + 1) * GQA_GROUP)
            q4 = jnp.concatenate([q_ref[rs, hd * HEAD_DIM:(hd + 1) * HEAD_DIM] for hd in heads], axis=0)
            s4 = lax.dot_general(q4, kb, (((1,), (1,)), ((), ())), preferred_element_type=_F32)
            ps, denoms = [], []
            for j, hd in enumerate(heads):
                s = jnp.where(valid, s4[j * BLOCK:(j + 1) * BLOCK] * scale, MASK_VALUE)
                sk = sink_ref[layer, hd]
                m = jnp.maximum(jnp.max(s, axis=-1, keepdims=True), sk)
                p = jnp.exp(s - m)
                denoms.append(jnp.sum(p, axis=-1, keepdims=True) + jnp.exp(sk - m))
                ps.append(p.astype(_BF16))
            o4 = _dot(jnp.concatenate(ps, axis=0), vb)
            for j, hd in enumerate(heads):
                attn_scr[rs, hd * HEAD_DIM:(hd + 1) * HEAD_DIM] = o4[j * BLOCK:(j + 1) * BLOCK] / denoms[j]
        for hd in range(n_gmlp):
            sl = slice(hd * HEAD_DIM, (hd + 1) * HEAD_DIM)
            f = _dot(ws_ref[hd], vn_ref[rs, sl]) + bst_ref[:, hd:hd + 1]
            sgu_scr[rs, sl] = u_ref[rs, sl].astype(_F32) * f

    group = MIX_GROUP_BLOCKS * BLOCK
    for r0 in range(0, tm, group):
        rows = slice(r0, r0 + group)
        mix_scr[rows, :attn_w] = _rms(attn_scr[rows, :], ag_ref[...]).astype(_BF16)
        mix_scr[rows, attn_w:] = _rms(sgu_scr[rows, :], sg_ref[...]).astype(_BF16)
        x_new = x_ref[rows, :] + _dot(mix_scr[rows, :], wo_ref[...])
        xo_ref[rows, :] = x_new
        hn_ref[rows, :] = _rms(x_new, g2_ref[...]).astype(_BF16)


def _mixer(layer, sink, q, k, v, u, vn, x2, wo, ws, bst, ag, sg, g2, *, batch, seq):
    tokens, d_model = x2.shape
    attn_w, kv_w, gmlp_w = q.shape[1], k.shape[1], u.shape[1]
    tm = MIX_TM
    tiles = seq // tm
    row = lambda w: pl.BlockSpec((tm, w), lambda b, t: (b * tiles + t, 0))
    whole_seq = pl.BlockSpec((seq, kv_w), lambda b, t: (b, 0))
    lb = functools.partial(_layer_block, layer=layer)
    body = functools.partial(_mixer_body, layer=layer, seq=seq, n_kv=kv_w // HEAD_DIM,
                             n_gmlp=gmlp_w // HEAD_DIM)
    return pl.pallas_call(
        body,
        grid=(batch, tiles),
        in_specs=[pl.BlockSpec(memory_space=pltpu.SMEM), row(attn_w), whole_seq, whole_seq, row(gmlp_w),
                  row(gmlp_w), row(d_model), lb(wo), lb(ws), lb(bst), lb(ag), lb(sg), lb(g2)],
        out_specs=[row(d_model), row(d_model)],
        out_shape=[jax.ShapeDtypeStruct((tokens, d_model), _F32),
                   jax.ShapeDtypeStruct((tokens, d_model), _BF16)],
        scratch_shapes=[pltpu.VMEM((tm, attn_w), _F32), pltpu.VMEM((tm, gmlp_w), _F32),
                        pltpu.VMEM((tm, d_model), _BF16)],
        compiler_params=pltpu.CompilerParams(dimension_semantics=("parallel", "parallel"),
                                             vmem_limit_bytes=VMEM_LIMIT_BYTES),
        name="mixer",
    )(sink, q, k, v, u, vn, x2, wo, ws, bst, ag, sg, g2)


def _ffn_body(prev_ref, main_ref, next_ref, x_ref, wg_ref, wu_ref, cwg_ref, cwu_ref, cbg_ref, cbu_ref,
              wd_ref, o_ref, win_scr, ag_scr, au_scr, *, tiles_per_seq):
    tm = main_ref.shape[0]
    m = pl.program_id(0)
    j = pl.program_id(1)

    @pl.when(j == 0)
    def _():
        t = m % tiles_per_seq
        win_scr[:HALO] = jnp.where(t == 0, jnp.zeros_like(prev_ref[...]), prev_ref[...])
        win_scr[HALO:HALO + tm] = main_ref[...]
        win_scr[HALO + tm:] = jnp.where(t == tiles_per_seq - 1, jnp.zeros_like(next_ref[...]), next_ref[...])
        o_ref[...] = x_ref[...]

    win = win_scr[...]
    slabs = ag_scr.shape[0]

    def up_project(w_ref, a_scr):
        a = _dot(win, w_ref[...])
        for s in range(slabs):
            a_scr[s] = a[:, s * LANES:(s + 1) * LANES]

    def conv(a_scr, cw_ref, cb_ref, s):
        cols = slice(s * LANES, (s + 1) * LANES)
        below = a_scr[s, pl.ds(HALO - 1, tm, stride=1), :]
        mid = a_scr[s, pl.ds(HALO, tm), :]
        above = a_scr[s, pl.ds(HALO + 1, tm, stride=1), :]
        return cb_ref[:, cols] + below * cw_ref[0:1, cols] + mid * cw_ref[1:2, cols] + above * cw_ref[2:3, cols]

    up_project(wg_ref, ag_scr)
    up_project(wu_ref, au_scr)
    acts = []
    for s in range(slabs):
        gate = conv(ag_scr, cwg_ref, cbg_ref, s)
        up = conv(au_scr, cwu_ref, cbu_ref, s)
        acts.append((gate * (1.0 / (1.0 + jnp.exp(-gate))) * up).astype(_BF16))
    o_ref[...] += _dot(jnp.concatenate(acts, axis=1), wd_ref[...])


def _ffn(layer, hn, x2, w_up, conv_w, conv_b, w_down, *, seq):
    tokens, d_model = x2.shape
    d_ff = w_down.shape[0]
    tm, tf = FFN_TM, FFN_TF
    tiles_per_seq = seq // tm
    halo_blocks = tm // HALO
    last_halo = tokens // HALO - 1
    u_off = d_ff // tf
    wrows = tm + 2 * HALO
    row = lambda: pl.BlockSpec((tm, d_model), lambda m, j: (m, 0))
    a_scratch = pltpu.VMEM((tf // LANES, wrows, LANES), _F32)
    return pl.pallas_call(
        functools.partial(_ffn_body, tiles_per_seq=tiles_per_seq),
        grid=(tokens // tm, d_ff // tf),
        in_specs=[
            pl.BlockSpec((HALO, d_model), lambda m, j: (jnp.maximum(m * halo_blocks - 1, 0), 0)),
            pl.BlockSpec((tm, d_model), lambda m, j: (m, 0), pipeline_mode=pl.Buffered(1)),
            pl.BlockSpec((HALO, d_model), lambda m, j: (jnp.minimum((m + 1) * halo_blocks, last_halo), 0)),
            pl.BlockSpec((tm, d_model), lambda m, j: (m, 0), pipeline_mode=pl.Buffered(1)),
            pl.BlockSpec((d_model, tf), lambda m, j: (0, j)),
            pl.BlockSpec((d_model, tf), lambda m, j: (0, j + u_off)),
            pl.BlockSpec((None, CONV_WIDTH, tf), lambda m, j: (layer, 0, j)),
            pl.BlockSpec((None, CONV_WIDTH, tf), lambda m, j: (layer, 0, j + u_off)),
            pl.BlockSpec((None, 1, tf), lambda m, j: (layer, 0, j)),
            pl.BlockSpec((None, 1, tf), lambda m, j: (layer, 0, j + u_off)),
            pl.BlockSpec((tf, d_model), lambda m, j: (j, 0)),
        ],
        out_specs=row(),
        out_shape=jax.ShapeDtypeStruct((tokens, d_model), _F32),
        scratch_shapes=[pltpu.VMEM((wrows, d_model), _BF16), a_scratch, a_scratch],
        compiler_params=pltpu.CompilerParams(dimension_semantics=("parallel", "arbitrary"),
                                             vmem_limit_bytes=VMEM_LIMIT_BYTES),
        name="ffn",
    )(hn, hn, hn, x2, w_up, w_up, conv_w, conv_w, conv_b, conv_b, w_down)


def _rope_tables(seq):
    inv_freq = ROPE_THETA ** (-jnp.arange(0, HEAD_DIM, 2, dtype=_F32) / HEAD_DIM)
    ang = jnp.arange(seq, dtype=_F32)[:, None] * inv_freq[None, :]
    cos, sin = jnp.cos(ang), jnp.sin(ang)
    return jnp.concatenate([cos, cos], axis=-1), jnp.concatenate([-sin, sin], axis=-1)


def kernel(x, norm1_g, w_in, q_norm_g, k_norm_g, sink, sgu_ln_g, sgu_ln_b, w_s, b_s,
           attn_out_g, sgu_out_g, w_o, norm2_g, w_up, conv_w, conv_b, w_down):
    batch, seq, d_model = x.shape
    depth = w_in.shape[0]
    attn_w = attn_out_g.shape[1]
    gmlp_w = sgu_out_g.shape[1]
    kv_w = (w_in.shape[2] - attn_w - 2 * gmlp_w) // 2
    assert seq % max(IN_TM, MIX_TM, FFN_TM) == 0 and w_down.shape[1] % FFN_TF == 0
    assert w_s.shape[2] == BLOCK and seq >= BAND and MIX_TM % (MIX_GROUP_BLOCKS * BLOCK) == 0

    cosf, sinf = _rope_tables(seq)
    rows = lambda a: a.reshape(depth, 1, -1)
    g1, qg, kg, lng, lnb = (rows(a) for a in (norm1_g, q_norm_g, k_norm_g, sgu_ln_g, sgu_ln_b))
    ag, sg, g2, cb = (rows(a) for a in (attn_out_g, sgu_out_g, norm2_g, conv_b))
    bst = jnp.swapaxes(b_s, 1, 2)
    win, wo, ws = (w.astype(_BF16) for w in (w_in, w_o, w_s))

    x2 = x.reshape(batch * seq, d_model)
    for l in range(depth):
        q, k, v, u, vn, wup, wdn = _in_proj(l, x2, g1, win, qg, kg, cosf, sinf, lng, lnb, w_up, w_down,
                                            seq=seq, attn_w=attn_w, kv_w=kv_w, gmlp_w=gmlp_w)
        x2, hn = _mixer(l, sink, q, k, v, u, vn, x2, wo, ws, bst, ag, sg, g2, batch=batch, seq=seq)
        x2 = _ffn(l, hn, x2, wup, conv_w, cb, wdn, seq=seq)
    return x2.reshape(batch, seq, d_model)
```

```python
import functools

import jax
import jax.numpy as jnp
import numpy as np
from jax import lax
from jax.experimental import pallas as pl
from jax.experimental.pallas import tpu as pltpu

HEAD_DIM = 128
GQA_GROUP = 4
WINDOW = 128
BLOCK = 128
BAND = 3 * BLOCK
CONV_WIDTH = 3
ROPE_THETA = 10000.0
EPS = 1e-6
MASK_VALUE = -1e30

BF16_TILE_ROWS = 16
LANES = 128
HALO = BF16_TILE_ROWS
VMEM_LIMIT_BYTES = 56 * 1024 * 1024

IN_TM = 512
MIX_TM = 512
MIX_GROUP_BLOCKS = 2
FFN_TM = 1024
FFN_TF = 512

_BF16 = jnp.bfloat16
_F32 = jnp.float32


def _dot(a, b):
    return jnp.dot(a, b, preferred_element_type=_F32)


def _rms(x, g):
    return x * lax.rsqrt(jnp.mean(x * x, axis=-1, keepdims=True) + EPS) * g


def _gelu_tanh(x):
    c = np.float32(np.sqrt(2.0 / np.pi))
    return x * (0.5 * (1.0 + jnp.tanh(c * (x + np.float32(0.044715) * (x * x * x)))))


def _layer_block(arr, layer):
    tail = (0,) * (arr.ndim - 1)
    return pl.BlockSpec((None,) + arr.shape[1:], lambda *_: (layer,) + tail, pipeline_mode=pl.Buffered(1))


def _in_proj_body(x_ref, g1_ref, win_ref, qg_ref, kg_ref, cos_ref, sin_ref, lng_ref, lnb_ref,
                  wup_ref, wdn_ref, q_ref, k_ref, v_ref, u_ref, vn_ref, wup_out_ref, wdn_out_ref,
                  *, attn_w, kv_w, gmlp_w):
    wup_out_ref[...] = wup_ref[...].astype(_BF16)
    wdn_out_ref[...] = wdn_ref[...].astype(_BF16)
    h = _rms(x_ref[...], g1_ref[...]).astype(_BF16)
    cosf = cos_ref[...]
    sinf = sin_ref[...]

    def qk_heads(col0, n_heads, gain, out_ref):
        y = _dot(h, win_ref[:, col0:col0 + n_heads * HEAD_DIM])
        for hd in range(n_heads):
            sl = slice(hd * HEAD_DIM, (hd + 1) * HEAD_DIM)
            t = _rms(y[:, sl], gain)
            out_ref[:, sl] = (t * cosf + pltpu.roll(t, HEAD_DIM // 2, 1) * sinf).astype(_BF16)

    c_v = attn_w + kv_w
    c_u = c_v + kv_w
    c_gv = c_u + gmlp_w
    qk_heads(0, attn_w // HEAD_DIM, qg_ref[...], q_ref)
    qk_heads(attn_w, kv_w // HEAD_DIM, kg_ref[...], k_ref)
    v_ref[...] = _dot(h, win_ref[:, c_v:c_v + kv_w]).astype(_BF16)
    u_ref[...] = _gelu_tanh(_dot(h, win_ref[:, c_u:c_u + gmlp_w])).astype(_BF16)
    gv = _gelu_tanh(_dot(h, win_ref[:, c_gv:c_gv + gmlp_w]))
    xc = gv - jnp.mean(gv, axis=-1, keepdims=True)
    vn = xc * lax.rsqrt(jnp.mean(xc * xc, axis=-1, keepdims=True) + EPS) * lng_ref[...] + lnb_ref[...]
    vn_ref[...] = vn.astype(_BF16)


def _in_proj(layer, x2, g1, win, qg, kg, cosf, sinf, lng, lnb, w_up, w_down, *, seq, attn_w, kv_w, gmlp_w):
    tokens, d_model = x2.shape
    tm = IN_TM
    steps = tokens // tm
    tiles_per_seq = seq // tm
    row = lambda w: pl.BlockSpec((tm, w), lambda m: (m, 0))
    pos = pl.BlockSpec((tm, HEAD_DIM), lambda m: (m % tiles_per_seq, 0))
    lb = functools.partial(_layer_block, layer=layer)
    assert w_up.shape[1] % (steps * BF16_TILE_ROWS) == 0 and w_down.shape[1] % (steps * BF16_TILE_ROWS) == 0
    up_rows, up_cols = w_up.shape[1] // steps, w_up.shape[2]
    dn_rows, dn_cols = w_down.shape[1] // steps, w_down.shape[2]
    return pl.pallas_call(
        functools.partial(_in_proj_body, attn_w=attn_w, kv_w=kv_w, gmlp_w=gmlp_w),
        grid=(steps,),
        in_specs=[row(d_model), lb(g1), lb(win), lb(qg), lb(kg), pos, pos, lb(lng), lb(lnb),
                  pl.BlockSpec((None, up_rows, up_cols), lambda m: (layer, m, 0)),
                  pl.BlockSpec((None, dn_rows, dn_cols), lambda m: (layer, m, 0))],
        out_specs=[row(attn_w), row(kv_w), row(kv_w), row(gmlp_w), row(gmlp_w),
                   pl.BlockSpec((up_rows, up_cols), lambda m: (m, 0)),
                   pl.BlockSpec((dn_rows, dn_cols), lambda m: (m, 0))],
        out_shape=[jax.ShapeDtypeStruct((tokens, w), _BF16) for w in (attn_w, kv_w, kv_w, gmlp_w, gmlp_w)]
        + [jax.ShapeDtypeStruct(w_up.shape[1:], _BF16), jax.ShapeDtypeStruct(w_down.shape[1:], _BF16)],
        compiler_params=pltpu.CompilerParams(dimension_semantics=("parallel",),
                                             vmem_limit_bytes=VMEM_LIMIT_BYTES),
        name="in_proj",
    )(x2, g1, win, qg, kg, cosf, sinf, lng, lnb, w_up, w_down)


def _mixer_body(sink_ref, q_ref, k_ref, v_ref, u_ref, vn_ref, x_ref, wo_ref, ws_ref, bst_ref,
                ag_ref, sg_ref, g2_ref, xo_ref, hn_ref, attn_scr, sgu_scr, mix_scr,
                *, layer, seq, n_kv, n_gmlp):
    tm = q_ref.shape[0]
    attn_w = q_ref.shape[1]
    blocks = tm // BLOCK
    first_block = pl.program_id(1) * blocks
    scale = np.float32(HEAD_DIM ** -0.5)

    for i in range(blocks):
        rs = slice(i * BLOCK, (i + 1) * BLOCK)
        n = first_block + i
        ks = pl.multiple_of(jnp.clip((n - 1) * BLOCK, 0, seq - BAND), BLOCK)
        rel = (ks - n * BLOCK) + (lax.broadcasted_iota(jnp.int32, (BLOCK, BAND), 1)
                                  - lax.broadcasted_iota(jnp.int32, (BLOCK, BAND), 0))
        valid = jnp.abs(rel) <= WINDOW
        for g in range(n_kv):
            kv_sl = slice(g * HEAD_DIM, (g + 1) * HEAD_DIM)
            kb = k_ref[pl.ds(ks, BAND), kv_sl]
            vb = v_ref[pl.ds(ks, BAND), kv_sl]
            heads = range(g * GQA_GROUP, (g + 1) * GQA_GROUP)
            q4 = jnp.concatenate([q_ref[rs, hd * HEAD_DIM:(hd + 1) * HEAD_DIM] for hd in heads], axis=0)
            s4 = lax.dot_general(q4, kb, (((1,), (1,)), ((), ())), preferred_element_type=_F32)
            ps, denoms = [], []
            for j, hd in enumerate(heads):
                s = jnp.where(valid, s4[j * BLOCK:(j + 1) * BLOCK] * scale, MASK_VALUE)
                sk = sink_ref[layer, hd]
                m = jnp.maximum(jnp.max(s, axis=-1, keepdims=True), sk)
                p = jnp.exp(s - m)
                denoms.append(jnp.sum(p, axis=-1, keepdims=True) + jnp.exp(sk - m))
                ps.append(p.astype(_BF16))
            o4 = _dot(jnp.concatenate(ps, axis=0), vb)
            for j, hd in enumerate(heads):
                attn_scr[rs, hd * HEAD_DIM:(hd + 1) * HEAD_DIM] = o4[j * BLOCK:(j + 1) * BLOCK] / denoms[j]
        for hd in range(n_gmlp):
            sl = slice(hd * HEAD_DIM, (hd + 1) * HEAD_DIM)
            f = _dot(ws_ref[hd], vn_ref[rs, sl]) + bst_ref[:, hd:hd + 1]
            sgu_scr[rs, sl] = u_ref[rs, sl].astype(_F32) * f

    group = MIX_GROUP_BLOCKS * BLOCK
    for r0 in range(0, tm, group):
        rows = slice(r0, r0 + group)
        mix_scr[rows, :attn_w] = _rms(attn_scr[rows, :], ag_ref[...]).astype(_BF16)
        mix_scr[rows, attn_w:] = _rms(sgu_scr[rows, :], sg_ref[...]).astype(_BF16)
        x_new = x_ref[rows, :] + _dot(mix_scr[rows, :], wo_ref[...])
        xo_ref[rows, :] = x_new
        hn_ref[rows, :] = _rms(x_new, g2_ref[...]).astype(_BF16)


def _mixer(layer, sink, q, k, v, u, vn, x2, wo, ws, bst, ag, sg, g2, *, batch, seq):
    tokens, d_model = x2.shape
    attn_w, kv_w, gmlp_w = q.shape[1], k.shape[1], u.shape[1]
    tm = MIX_TM
    tiles = seq // tm
    row = lambda w: pl.BlockSpec((tm, w), lambda b, t: (b * tiles + t, 0))
    whole_seq = pl.BlockSpec((seq, kv_w), lambda b, t: (b, 0))
    lb = functools.partial(_layer_block, layer=layer)
    body = functools.partial(_mixer_body, layer=layer, seq=seq, n_kv=kv_w // HEAD_DIM,
                             n_gmlp=gmlp_w // HEAD_DIM)
    return pl.pallas_call(
        body,
        grid=(batch, tiles),
        in_specs=[pl.BlockSpec(memory_space=pltpu.SMEM), row(attn_w), whole_seq, whole_seq, row(gmlp_w),
                  row(gmlp_w), row(d_model), lb(wo), lb(ws), lb(bst), lb(ag), lb(sg), lb(g2)],
        out_specs=[row(d_model), row(d_model)],
        out_shape=[jax.ShapeDtypeStruct((tokens, d_model), _F32),
                   jax.ShapeDtypeStruct((tokens, d_model), _BF16)],
        scratch_shapes=[pltpu.VMEM((tm, attn_w), _F32), pltpu.VMEM((tm, gmlp_w), _F32),
                        pltpu.VMEM((tm, d_model), _BF16)],
        compiler_params=pltpu.CompilerParams(dimension_semantics=("parallel", "parallel"),
                                             vmem_limit_bytes=VMEM_LIMIT_BYTES),
        name="mixer",
    )(sink, q, k, v, u, vn, x2, wo, ws, bst, ag, sg, g2)


def _ffn_body(prev_ref, hn_hbm, next_ref, x_hbm, wg_ref, wu_ref, cwg_ref, cwu_ref, cbg_ref, cbu_ref,
              wd_ref, o_ref, win_scr, ag_scr, au_scr, main_scr, x_scr, sems, *, tiles_per_seq, n_tiles):
    tm = main_scr.shape[0]
    m = pl.program_id(0)
    j = pl.program_id(1)

    def tile_copies(tile):
        rows = pl.ds(pl.multiple_of(tile * tm, tm), tm)
        return (pltpu.make_async_copy(hn_hbm.at[rows, :], main_scr, sems.at[0]),
                pltpu.make_async_copy(x_hbm.at[rows, :], x_scr, sems.at[1]))

    @pl.when(jnp.logical_and(m == 0, j == 0))
    def _():
        for copy in tile_copies(m):
            copy.start()

    @pl.when(j == 0)
    def _():
        for copy in tile_copies(m):
            copy.wait()
        t = m % tiles_per_seq
        win_scr[:HALO] = jnp.where(t == 0, jnp.zeros_like(prev_ref[...]), prev_ref[...])
        win_scr[HALO:HALO + tm] = main_scr[...]
        win_scr[HALO + tm:] = jnp.where(t == tiles_per_seq - 1, jnp.zeros_like(next_ref[...]), next_ref[...])
        o_ref[...] = x_scr[...]

    @pl.when(jnp.logical_and(j == 1, m + 1 < n_tiles))
    def _():
        for copy in tile_copies(m + 1):
            copy.start()

    win = win_scr[...]
    slabs = ag_scr.shape[0]

    def up_project(w_ref, a_scr):
        a = _dot(win, w_ref[...])
        for s in range(slabs):
            a_scr[s] = a[:, s * LANES:(s + 1) * LANES]

    def conv(a_scr, cw_ref, cb_ref, s):
        cols = slice(s * LANES, (s + 1) * LANES)
        below = a_scr[s, pl.ds(HALO - 1, tm, stride=1), :]
        mid = a_scr[s, pl.ds(HALO, tm), :]
        above = a_scr[s, pl.ds(HALO + 1, tm, stride=1), :]
        return cb_ref[:, cols] + below * cw_ref[0:1, cols] + mid * cw_ref[1:2, cols] + above * cw_ref[2:3, cols]

    up_project(wg_ref, ag_scr)
    up_project(wu_ref, au_scr)
    acts = []
    for s in range(slabs):
        gate = conv(ag_scr, cwg_ref, cbg_ref, s)
        up = conv(au_scr, cwu_ref, cbu_ref, s)
        acts.append((gate * (1.0 / (1.0 + jnp.exp(-gate))) * up).astype(_BF16))
    o_ref[...] += _dot(jnp.concatenate(acts, axis=1), wd_ref[...])


def _ffn(layer, hn, x2, w_up, conv_w, conv_b, w_down, *, seq):
    tokens, d_model = x2.shape
    d_ff = w_down.shape[0]
    tm, tf = FFN_TM, FFN_TF
    tiles_per_seq = seq // tm
    halo_blocks = tm // HALO
    last_halo = tokens // HALO - 1
    u_off = d_ff // tf
    wrows = tm + 2 * HALO
    n_tiles = tokens // tm
    assert d_ff // tf >= 2
    a_scratch = pltpu.VMEM((tf // LANES, wrows, LANES), _F32)
    return pl.pallas_call(
        functools.partial(_ffn_body, tiles_per_seq=tiles_per_seq, n_tiles=n_tiles),
        grid=(n_tiles, d_ff // tf),
        in_specs=[
            pl.BlockSpec((HALO, d_model), lambda m, j: (jnp.maximum(m * halo_blocks - 1, 0), 0)),
            pl.BlockSpec(memory_space=pl.ANY),
            pl.BlockSpec((HALO, d_model), lambda m, j: (jnp.minimum((m + 1) * halo_blocks, last_halo), 0)),
            pl.BlockSpec(memory_space=pl.ANY),
            pl.BlockSpec((d_model, tf), lambda m, j: (0, j)),
            pl.BlockSpec((d_model, tf), lambda m, j: (0, j + u_off)),
            pl.BlockSpec((None, CONV_WIDTH, tf), lambda m, j: (layer, 0, j)),
            pl.BlockSpec((None, CONV_WIDTH, tf), lambda m, j: (layer, 0, j + u_off)),
            pl.BlockSpec((None, 1, tf), lambda m, j: (layer, 0, j)),
            pl.BlockSpec((None, 1, tf), lambda m, j: (layer, 0, j + u_off)),
            pl.BlockSpec((tf, d_model), lambda m, j: (j, 0)),
        ],
        out_specs=pl.BlockSpec((tm, d_model), lambda m, j: (m, 0)),
        out_shape=jax.ShapeDtypeStruct((tokens, d_model), _F32),
        scratch_shapes=[pltpu.VMEM((wrows, d_model), _BF16), a_scratch, a_scratch,
                        pltpu.VMEM((tm, d_model), _BF16), pltpu.VMEM((tm, d_model), _F32),
                        pltpu.SemaphoreType.DMA((2,))],
        compiler_params=pltpu.CompilerParams(dimension_semantics=("arbitrary", "arbitrary"),
                                             vmem_limit_bytes=VMEM_LIMIT_BYTES),
        name="ffn",
    )(hn, hn, hn, x2, w_up, w_up, conv_w, conv_w, conv_b, conv_b, w_down)


def _rope_tables(seq):
    inv_freq = ROPE_THETA ** (-jnp.arange(0, HEAD_DIM, 2, dtype=_F32) / HEAD_DIM)
    ang = jnp.arange(seq, dtype=_F32)[:, None] * inv_freq[None, :]
    cos, sin = jnp.cos(ang), jnp.sin(ang)
    return jnp.concatenate([cos, cos], axis=-1), jnp.concatenate([-sin, sin], axis=-1)


def kernel(x, norm1_g, w_in, q_norm_g, k_norm_g, sink, sgu_ln_g, sgu_ln_b, w_s, b_s,
           attn_out_g, sgu_out_g, w_o, norm2_g, w_up, conv_w, conv_b, w_down):
    batch, seq, d_model = x.shape
    depth = w_in.shape[0]
    attn_w = attn_out_g.shape[1]
    gmlp_w = sgu_out_g.shape[1]
    kv_w = (w_in.shape[2] - attn_w - 2 * gmlp_w) // 2
    assert seq % max(IN_TM, MIX_TM, FFN_TM) == 0 and w_down.shape[1] % FFN_TF == 0
    assert w_s.shape[2] == BLOCK and seq >= BAND and MIX_TM % (MIX_GROUP_BLOCKS * BLOCK) == 0

    cosf, sinf = _rope_tables(seq)
    rows = lambda a: a.reshape(depth, 1, -1)
    g1, qg, kg, lng, lnb = (rows(a) for a in (norm1_g, q_norm_g, k_norm_g, sgu_ln_g, sgu_ln_b))
    ag, sg, g2, cb = (rows(a) for a in (attn_out_g, sgu_out_g, norm2_g, conv_b))
    bst = jnp.swapaxes(b_s, 1, 2)
    win, wo, ws = (w.astype(_BF16) for w in (w_in, w_o, w_s))

    x2 = x.reshape(batch * seq, d_model)
    for l in range(depth):
        q, k, v, u, vn, wup, wdn = _in_proj(l, x2, g1, win, qg, kg, cosf, sinf, lng, lnb, w_up, w_down,
                                            seq=seq, attn_w=attn_w, kv_w=kv_w, gmlp_w=gmlp_w)
        x2, hn = _mixer(l, sink, q, k, v, u, vn, x2, wo, ws, bst, ag, sg, g2, batch=batch, seq=seq)
        x2 = _ffn(l, hn, x2, wup, conv_w, cb, wdn, seq=seq)
    return x2.reshape(batch, seq, d_model)
```

```python
import functools

import jax
import jax.numpy as jnp
import numpy as np
from jax import lax
from jax.experimental import pallas as pl
from jax.experimental.pallas import tpu as pltpu

HEAD_DIM = 128
GQA_GROUP = 4
WINDOW = 128
BLOCK = 128
BAND = 3 * BLOCK
CONV_WIDTH = 3
ROPE_THETA = 10000.0
EPS = 1e-6
MASK_VALUE = -1e30
LOG2E = 1.4426950408889634

BF16_TILE_ROWS = 16
LANES = 128
HALO = BF16_TILE_ROWS
VMEM_LIMIT_BYTES = 56 * 1024 * 1024

IN_TM = 512
MIX_TM = 512
MIX_GROUP_BLOCKS = 2
FFN_TM = 1024
FFN_TF = 512

_BF16 = jnp.bfloat16
_F32 = jnp.float32


def _dot(a, b):
    return jnp.dot(a, b, preferred_element_type=_F32)


def _rms(x, g):
    return x * lax.rsqrt(jnp.mean(x * x, axis=-1, keepdims=True) + EPS) * g


def _gelu_tanh(x):
    c = np.float32(np.sqrt(2.0 / np.pi))
    return x * (0.5 * (1.0 + jnp.tanh(c * (x + np.float32(0.044715) * (x * x * x)))))


def _resident(arr):
    zeros = (0,) * arr.ndim
    return pl.BlockSpec(arr.shape, lambda *_: zeros, pipeline_mode=pl.Buffered(1))


def _layer_block(arr, layer):
    tail = (0,) * (arr.ndim - 1)
    return pl.BlockSpec((None,) + arr.shape[1:], lambda *_: (layer,) + tail, pipeline_mode=pl.Buffered(1))


def _in_proj_body(x_ref, g1_ref, win_ref, qg_ref, kg_ref, cos_ref, sin_ref, lng_ref, lnb_ref, *refs,
                  n_cast, attn_w, kv_w, gmlp_w):
    cast_in = refs[:n_cast]
    q_ref, k_ref, v_ref, u_ref, vn_ref = refs[n_cast:n_cast + 5]
    cast_out = refs[n_cast + 5:]
    for src, dst in zip(cast_in, cast_out):
        dst[...] = src[...].astype(_BF16)
    h = _rms(x_ref[...], g1_ref[...]).astype(_BF16)
    cosf = cos_ref[...]
    sinf = sin_ref[...]

    def qk_heads(col0, n_heads, gain, out_ref):
        y = _dot(h, win_ref[:, col0:col0 + n_heads * HEAD_DIM])
        for hd in range(n_heads):
            sl = slice(hd * HEAD_DIM, (hd + 1) * HEAD_DIM)
            t = _rms(y[:, sl], gain)
            out_ref[:, sl] = (t * cosf + pltpu.roll(t, HEAD_DIM // 2, 1) * sinf).astype(_BF16)

    c_v = attn_w + kv_w
    c_u = c_v + kv_w
    c_gv = c_u + gmlp_w
    qk_heads(0, attn_w // HEAD_DIM, qg_ref[...], q_ref)
    qk_heads(attn_w, kv_w // HEAD_DIM, kg_ref[...], k_ref)
    v_ref[...] = _dot(h, win_ref[:, c_v:c_v + kv_w]).astype(_BF16)
    u_ref[...] = _gelu_tanh(_dot(h, win_ref[:, c_u:c_u + gmlp_w])).astype(_BF16)
    gv = _gelu_tanh(_dot(h, win_ref[:, c_gv:c_gv + gmlp_w]))
    xc = gv - jnp.mean(gv, axis=-1, keepdims=True)
    vn = xc * lax.rsqrt(jnp.mean(xc * xc, axis=-1, keepdims=True) + EPS) * lng_ref[...] + lnb_ref[...]
    vn_ref[...] = vn.astype(_BF16)


def _in_proj(layer, x2, g1, win, qg, kg, cosf, sinf, lng, lnb, to_cast, *, seq, attn_w, kv_w, gmlp_w):
    tokens, d_model = x2.shape
    tm = IN_TM
    steps = tokens // tm
    tiles_per_seq = seq // tm
    row = lambda w: pl.BlockSpec((tm, w), lambda m: (m, 0))
    pos = pl.BlockSpec((tm, HEAD_DIM), lambda m: (m % tiles_per_seq, 0))
    lb = functools.partial(_layer_block, layer=layer)
    cast_in, cast_out, cast_shapes = [], [], []
    for w, wl in to_cast:
        assert w.shape[1] % (steps * BF16_TILE_ROWS) == 0
        slab = (w.shape[1] // steps, w.shape[2])
        cast_in.append(pl.BlockSpec((None,) + slab, lambda m, wl=wl: (wl, m, 0)))
        cast_out.append(pl.BlockSpec(slab, lambda m: (m, 0)))
        cast_shapes.append(jax.ShapeDtypeStruct(w.shape[1:], _BF16))
    return pl.pallas_call(
        functools.partial(_in_proj_body, n_cast=len(to_cast), attn_w=attn_w, kv_w=kv_w, gmlp_w=gmlp_w),
        grid=(steps,),
        in_specs=[row(d_model), lb(g1), _resident(win), lb(qg), lb(kg), pos, pos, lb(lng), lb(lnb)] + cast_in,
        out_specs=[row(attn_w), row(kv_w), row(kv_w), row(gmlp_w), row(gmlp_w)] + cast_out,
        out_shape=[jax.ShapeDtypeStruct((tokens, w), _BF16) for w in (attn_w, kv_w, kv_w, gmlp_w, gmlp_w)]
        + cast_shapes,
        compiler_params=pltpu.CompilerParams(dimension_semantics=("parallel",),
                                             vmem_limit_bytes=VMEM_LIMIT_BYTES),
        name="in_proj",
    )(x2, g1, win, qg, kg, cosf, sinf, lng, lnb, *(w for w, _ in to_cast))


def _mixer_body(sink_ref, q_ref, k_ref, v_ref, u_ref, vn_ref, x_ref, wo_ref, ws_ref, bst_ref,
                ag_ref, sg_ref, g2_ref, xo_ref, hn_ref, attn_scr, sgu_scr, mix_scr,
                *, layer, seq, n_kv, n_gmlp):
    tm = q_ref.shape[0]
    attn_w = q_ref.shape[1]
    blocks = tm // BLOCK
    first_block = pl.program_id(1) * blocks
    scale_log2e = HEAD_DIM ** -0.5 * LOG2E

    for i in range(blocks):
        rs = slice(i * BLOCK, (i + 1) * BLOCK)
        n = first_block + i
        ks = pl.multiple_of(jnp.clip((n - 1) * BLOCK, 0, seq - BAND), BLOCK)
        rel = (ks - n * BLOCK) + (lax.broadcasted_iota(jnp.int32, (BLOCK, BAND), 1)
                                  - lax.broadcasted_iota(jnp.int32, (BLOCK, BAND), 0))
        valid = jnp.abs(rel) <= WINDOW
        for g in range(n_kv):
            kv_sl = slice(g * HEAD_DIM, (g + 1) * HEAD_DIM)
            kb = k_ref[pl.ds(ks, BAND), kv_sl]
            vb = v_ref[pl.ds(ks, BAND), kv_sl]
            heads = range(g * GQA_GROUP, (g + 1) * GQA_GROUP)
            q4 = jnp.concatenate([q_ref[rs, hd * HEAD_DIM:(hd + 1) * HEAD_DIM] for hd in heads], axis=0)
            s4 = lax.dot_general(q4, kb, (((1,), (1,)), ((), ())), preferred_element_type=_F32)
            ps, denoms = [], []
            for j, hd in enumerate(heads):
                s = jnp.where(valid, s4[j * BLOCK:(j + 1) * BLOCK] * scale_log2e, MASK_VALUE)
                sk = sink_ref[layer, hd] * LOG2E
                m = jnp.maximum(jnp.max(s, axis=-1, keepdims=True), sk)
                p = jnp.exp2(s - m)
                denoms.append(jnp.sum(p, axis=-1, keepdims=True) + jnp.exp2(sk - m))
                ps.append(p.astype(_BF16))
            o4 = _dot(jnp.concatenate(ps, axis=0), vb)
            for j, hd in enumerate(heads):
                attn_scr[rs, hd * HEAD_DIM:(hd + 1) * HEAD_DIM] = o4[j * BLOCK:(j + 1) * BLOCK] / denoms[j]
        for hd in range(n_gmlp):
            sl = slice(hd * HEAD_DIM, (hd + 1) * HEAD_DIM)
            f = _dot(ws_ref[hd], vn_ref[rs, sl]) + bst_ref[:, hd:hd + 1]
            sgu_scr[rs, sl] = u_ref[rs, sl].astype(_F32) * f

    group = MIX_GROUP_BLOCKS * BLOCK
    for r0 in range(0, tm, group):
        rows = slice(r0, r0 + group)
        mix_scr[rows, :attn_w] = _rms(attn_scr[rows, :], ag_ref[...]).astype(_BF16)
        mix_scr[rows, attn_w:] = _rms(sgu_scr[rows, :], sg_ref[...]).astype(_BF16)
        x_new = x_ref[rows, :] + _dot(mix_scr[rows, :], wo_ref[...])
        xo_ref[rows, :] = x_new
        hn_ref[rows, :] = _rms(x_new, g2_ref[...]).astype(_BF16)


def _mixer(layer, sink, q, k, v, u, vn, x2, wo, ws, bst, ag, sg, g2, *, batch, seq):
    tokens, d_model = x2.shape
    attn_w, kv_w, gmlp_w = q.shape[1], k.shape[1], u.shape[1]
    tm = MIX_TM
    tiles = seq // tm
    row = lambda w: pl.BlockSpec((tm, w), lambda b, t: (b * tiles + t, 0))
    whole_seq = pl.BlockSpec((seq, kv_w), lambda b, t: (b, 0))
    lb = functools.partial(_layer_block, layer=layer)
    body = functools.partial(_mixer_body, layer=layer, seq=seq, n_kv=kv_w // HEAD_DIM,
                             n_gmlp=gmlp_w // HEAD_DIM)
    return pl.pallas_call(
        body,
        grid=(batch, tiles),
        in_specs=[pl.BlockSpec(memory_space=pltpu.SMEM), row(attn_w), whole_seq, whole_seq, row(gmlp_w),
                  row(gmlp_w), row(d_model), _resident(wo), lb(ws), lb(bst), lb(ag), lb(sg), lb(g2)],
        out_specs=[row(d_model), row(d_model)],
        out_shape=[jax.ShapeDtypeStruct((tokens, d_model), _F32),
                   jax.ShapeDtypeStruct((tokens, d_model), _BF16)],
        scratch_shapes=[pltpu.VMEM((tm, attn_w), _F32), pltpu.VMEM((tm, gmlp_w), _F32),
                        pltpu.VMEM((tm, d_model), _BF16)],
        compiler_params=pltpu.CompilerParams(dimension_semantics=("parallel", "parallel"),
                                             vmem_limit_bytes=VMEM_LIMIT_BYTES),
        name="mixer",
    )(sink, q, k, v, u, vn, x2, wo, ws, bst, ag, sg, g2)


def _ffn_body(prev_ref, hn_hbm, next_ref, x_hbm, wg_ref, wu_ref, cwg_ref, cwu_ref, cbg_ref, cbu_ref,
              wd_ref, o_ref, win_scr, ag_scr, au_scr, main_scr, x_scr, sems, *, tiles_per_seq, n_tiles):
    tm = main_scr.shape[0]
    m = pl.program_id(0)
    j = pl.program_id(1)

    def tile_copies(tile):
        rows = pl.ds(pl.multiple_of(tile * tm, tm), tm)
        return (pltpu.make_async_copy(hn_hbm.at[rows, :], main_scr, sems.at[0]),
                pltpu.make_async_copy(x_hbm.at[rows, :], x_scr, sems.at[1]))

    @pl.when(jnp.logical_and(m == 0, j == 0))
    def _():
        for copy in tile_copies(m):
            copy.start()

    @pl.when(j == 0)
    def _():
        for copy in tile_copies(m):
            copy.wait()
        t = m % tiles_per_seq
        win_scr[:HALO] = jnp.where(t == 0, jnp.zeros_like(prev_ref[...]), prev_ref[...])
        win_scr[HALO:HALO + tm] = main_scr[...]
        win_scr[HALO + tm:] = jnp.where(t == tiles_per_seq - 1, jnp.zeros_like(next_ref[...]), next_ref[...])
        o_ref[...] = x_scr[...]

    @pl.when(jnp.logical_and(j == 1, m + 1 < n_tiles))
    def _():
        for copy in tile_copies(m + 1):
            copy.start()

    win = win_scr[...]
    slabs = ag_scr.shape[0]

    def up_project(w_ref, a_scr):
        a = _dot(win, w_ref[...])
        for s in range(slabs):
            a_scr[s] = a[:, s * LANES:(s + 1) * LANES]

    def conv(a_scr, cw_ref, cb_ref, s):
        cols = slice(s * LANES, (s + 1) * LANES)
        below = a_scr[s, pl.ds(HALO - 1, tm, stride=1), :]
        mid = a_scr[s, pl.ds(HALO, tm), :]
        above = a_scr[s, pl.ds(HALO + 1, tm, stride=1), :]
        return cb_ref[:, cols] + below * cw_ref[0:1, cols] + mid * cw_ref[1:2, cols] + above * cw_ref[2:3, cols]

    up_project(wg_ref, ag_scr)
    up_project(wu_ref, au_scr)
    acts = []
    for s in range(slabs):
        gate = conv(ag_scr, cwg_ref, cbg_ref, s)
        up = conv(au_scr, cwu_ref, cbu_ref, s)
        acts.append((gate * (1.0 / (1.0 + jnp.exp(-gate))) * up).astype(_BF16))
    o_ref[...] += _dot(jnp.concatenate(acts, axis=1), wd_ref[...])


def _ffn(layer, hn, x2, w_up, conv_w, conv_b, w_down, *, seq):
    tokens, d_model = x2.shape
    d_ff = w_down.shape[0]
    tm, tf = FFN_TM, FFN_TF
    tiles_per_seq = seq // tm
    halo_blocks = tm // HALO
    last_halo = tokens // HALO - 1
    u_off = d_ff // tf
    wrows = tm + 2 * HALO
    n_tiles = tokens // tm
    assert d_ff // tf >= 2
    a_scratch = pltpu.VMEM((tf // LANES, wrows, LANES), _F32)
    return pl.pallas_call(
        functools.partial(_ffn_body, tiles_per_seq=tiles_per_seq, n_tiles=n_tiles),
        grid=(n_tiles, d_ff // tf),
        in_specs=[
            pl.BlockSpec((HALO, d_model), lambda m, j: (jnp.maximum(m * halo_blocks - 1, 0), 0)),
            pl.BlockSpec(memory_space=pl.ANY),
            pl.BlockSpec((HALO, d_model), lambda m, j: (jnp.minimum((m + 1) * halo_blocks, last_halo), 0)),
            pl.BlockSpec(memory_space=pl.ANY),
            pl.BlockSpec((d_model, tf), lambda m, j: (0, j)),
            pl.BlockSpec((d_model, tf), lambda m, j: (0, j + u_off)),
            pl.BlockSpec((None, CONV_WIDTH, tf), lambda m, j: (layer, 0, j)),
            pl.BlockSpec((None, CONV_WIDTH, tf), lambda m, j: (layer, 0, j + u_off)),
            pl.BlockSpec((None, 1, tf), lambda m, j: (layer, 0, j)),
            pl.BlockSpec((None, 1, tf), lambda m, j: (layer, 0, j + u_off)),
            pl.BlockSpec((tf, d_model), lambda m, j: (j, 0)),
        ],
        out_specs=pl.BlockSpec((tm, d_model), lambda m, j: (m, 0)),
        out_shape=jax.ShapeDtypeStruct((tokens, d_model), _F32),
        scratch_shapes=[pltpu.VMEM((wrows, d_model), _BF16), a_scratch, a_scratch,
                        pltpu.VMEM((tm, d_model), _BF16), pltpu.VMEM((tm, d_model), _F32),
                        pltpu.SemaphoreType.DMA((2,))],
        compiler_params=pltpu.CompilerParams(dimension_semantics=("arbitrary", "arbitrary"),
                                             vmem_limit_bytes=VMEM_LIMIT_BYTES),
        name="ffn",
    )(hn, hn, hn, x2, w_up, w_up, conv_w, conv_w, conv_b, conv_b, w_down)


def _rope_tables(seq):
    inv_freq = ROPE_THETA ** (-jnp.arange(0, HEAD_DIM, 2, dtype=_F32) / HEAD_DIM)
    ang = jnp.arange(seq, dtype=_F32)[:, None] * inv_freq[None, :]
    cos, sin = jnp.cos(ang), jnp.sin(ang)
    return jnp.concatenate([cos, cos], axis=-1), jnp.concatenate([-sin, sin], axis=-1)


def kernel(x, norm1_g, w_in, q_norm_g, k_norm_g, sink, sgu_ln_g, sgu_ln_b, w_s, b_s,
           attn_out_g, sgu_out_g, w_o, norm2_g, w_up, conv_w, conv_b, w_down):
    batch, seq, d_model = x.shape
    depth = w_in.shape[0]
    attn_w = attn_out_g.shape[1]
    gmlp_w = sgu_out_g.shape[1]
    kv_w = (w_in.shape[2] - attn_w - 2 * gmlp_w) // 2
    assert seq % max(IN_TM, MIX_TM, FFN_TM) == 0 and w_down.shape[1] % FFN_TF == 0
    assert w_s.shape[2] == BLOCK and seq >= BAND and MIX_TM % (MIX_GROUP_BLOCKS * BLOCK) == 0

    cosf, sinf = _rope_tables(seq)
    rows = lambda a: a.reshape(depth, 1, -1)
    g1, qg, kg, lng, lnb = (rows(a) for a in (norm1_g, q_norm_g, k_norm_g, sgu_ln_g, sgu_ln_b))
    ag, sg, g2, cb = (rows(a) for a in (attn_out_g, sgu_out_g, norm2_g, conv_b))
    bst = jnp.swapaxes(b_s, 1, 2)
    ws = w_s.astype(_BF16)
    win = w_in[0].astype(_BF16)

    x2 = x.reshape(batch * seq, d_model)
    for l in range(depth):
        to_cast = [(w_o, l), (w_up, l), (w_down, l)] + ([(w_in, l + 1)] if l + 1 < depth else [])
        q, k, v, u, vn, wo, wup, wdn, *nxt = _in_proj(l, x2, g1, win, qg, kg, cosf, sinf, lng, lnb, to_cast,
                                                      seq=seq, attn_w=attn_w, kv_w=kv_w, gmlp_w=gmlp_w)
        win = nxt[0] if nxt else None
        x2, hn = _mixer(l, sink, q, k, v, u, vn, x2, wo, ws, bst, ag, sg, g2, batch=batch, seq=seq)
        x2 = _ffn(l, hn, x2, wup, conv_w, cb, wdn, seq=seq)
    return x2.reshape(batch, seq, d_model)
```

```python
import functools

import jax
import jax.numpy as jnp
import numpy as np
from jax import lax
from jax.experimental import pallas as pl
from jax.experimental.pallas import tpu as pltpu

HEAD_DIM = 128
GQA_GROUP = 4
WINDOW = 128
BLOCK = 128
BAND = 3 * BLOCK
CONV_WIDTH = 3
ROPE_THETA = 10000.0
EPS = 1e-6
MASK_VALUE = -1e30
LOG2E = 1.4426950408889634

BF16_TILE_ROWS = 16
LANES = 128
HALO = BF16_TILE_ROWS
VMEM_LIMIT_BYTES = 56 * 1024 * 1024

IN_TM = 512
MIX_TM = 512
MIX_GROUP_BLOCKS = 2
FFN_TM = 1024
FFN_TF = 512

_BF16 = jnp.bfloat16
_F32 = jnp.float32


def _dot(a, b):
    return jnp.dot(a, b, preferred_element_type=_F32)


def _rms(x, g):
    return x * lax.rsqrt(jnp.mean(x * x, axis=-1, keepdims=True) + EPS) * g


def _gelu_tanh(x):
    c = np.float32(np.sqrt(2.0 / np.pi))
    return x * (0.5 * (1.0 + jnp.tanh(c * (x + np.float32(0.044715) * (x * x * x)))))


def _resident(arr):
    zeros = (0,) * arr.ndim
    return pl.BlockSpec(arr.shape, lambda *_: zeros, pipeline_mode=pl.Buffered(1))


def _layer_block(arr, layer):
    tail = (0,) * (arr.ndim - 1)
    return pl.BlockSpec((None,) + arr.shape[1:], lambda *_: (layer,) + tail, pipeline_mode=pl.Buffered(1))


def _in_proj_body(x_ref, g1_ref, win_ref, qg_ref, kg_ref, cos_ref, sin_ref, lng_ref, lnb_ref, *refs,
                  n_cast, attn_w, kv_w, gmlp_w):
    cast_in = refs[:n_cast]
    q_ref, k_ref, v_ref, u_ref, vn_ref = refs[n_cast:n_cast + 5]
    cast_out = refs[n_cast + 5:]
    for src, dst in zip(cast_in, cast_out):
        dst[...] = src[...].astype(_BF16)
    h = _rms(x_ref[...], g1_ref[...]).astype(_BF16)
    cosf = cos_ref[...]
    sinf = sin_ref[...]

    def qk_heads(col0, n_heads, gain, out_ref):
        y = _dot(h, win_ref[:, col0:col0 + n_heads * HEAD_DIM])
        for hd in range(n_heads):
            sl = slice(hd * HEAD_DIM, (hd + 1) * HEAD_DIM)
            t = _rms(y[:, sl], gain)
            out_ref[:, sl] = (t * cosf + pltpu.roll(t, HEAD_DIM // 2, 1) * sinf).astype(_BF16)

    c_v = attn_w + kv_w
    c_u = c_v + kv_w
    c_gv = c_u + gmlp_w
    qk_heads(0, attn_w // HEAD_DIM, qg_ref[...], q_ref)
    qk_heads(attn_w, kv_w // HEAD_DIM, kg_ref[...], k_ref)
    v_ref[...] = _dot(h, win_ref[:, c_v:c_v + kv_w]).astype(_BF16)
    gv = _gelu_tanh(_dot(h, win_ref[:, c_gv:c_gv + gmlp_w]))
    xc = gv - jnp.mean(gv, axis=-1, keepdims=True)
    vn = xc * lax.rsqrt(jnp.mean(xc * xc, axis=-1, keepdims=True) + EPS) * lng_ref[...] + lnb_ref[...]
    vn_ref[...] = vn.astype(_BF16)
    half = gmlp_w // 2
    for c0 in (0, half):
        u_ref[:, c0:c0 + half] = _gelu_tanh(_dot(h, win_ref[:, c_u + c0:c_u + c0 + half])).astype(_BF16)


def _in_proj(layer, x2, g1, win, qg, kg, cosf, sinf, lng, lnb, to_cast, *, seq, attn_w, kv_w, gmlp_w):
    tokens, d_model = x2.shape
    tm = IN_TM
    steps = tokens // tm
    tiles_per_seq = seq // tm
    row = lambda w: pl.BlockSpec((tm, w), lambda m: (m, 0))
    pos = pl.BlockSpec((tm, HEAD_DIM), lambda m: (m % tiles_per_seq, 0))
    lb = functools.partial(_layer_block, layer=layer)
    cast_in, cast_out, cast_shapes = [], [], []
    for w, wl in to_cast:
        assert w.shape[1] % (steps * BF16_TILE_ROWS) == 0
        slab = (w.shape[1] // steps, w.shape[2])
        cast_in.append(pl.BlockSpec((None,) + slab, lambda m, wl=wl: (wl, m, 0)))
        cast_out.append(pl.BlockSpec(slab, lambda m: (m, 0)))
        cast_shapes.append(jax.ShapeDtypeStruct(w.shape[1:], _BF16))
    return pl.pallas_call(
        functools.partial(_in_proj_body, n_cast=len(to_cast), attn_w=attn_w, kv_w=kv_w, gmlp_w=gmlp_w),
        grid=(steps,),
        in_specs=[row(d_model), lb(g1), _resident(win), lb(qg), lb(kg), pos, pos, lb(lng), lb(lnb)] + cast_in,
        out_specs=[row(attn_w), row(kv_w), row(kv_w), row(gmlp_w), row(gmlp_w)] + cast_out,
        out_shape=[jax.ShapeDtypeStruct((tokens, w), _BF16) for w in (attn_w, kv_w, kv_w, gmlp_w, gmlp_w)]
        + cast_shapes,
        compiler_params=pltpu.CompilerParams(dimension_semantics=("parallel",),
                                             vmem_limit_bytes=VMEM_LIMIT_BYTES),
        name="in_proj",
    )(x2, g1, win, qg, kg, cosf, sinf, lng, lnb, *(w for w, _ in to_cast))


def _mixer_body(sink_ref, q_ref, k_ref, v_ref, u_ref, vn_ref, x_ref, wo_ref, ws_ref, bst_ref,
                ag_ref, sg_ref, g2_ref, xo_ref, hn_ref, attn_scr, sgu_scr, mix_scr,
                *, layer, seq, n_kv, n_gmlp):
    tm = q_ref.shape[0]
    attn_w = q_ref.shape[1]
    blocks = tm // BLOCK
    first_block = pl.program_id(1) * blocks
    scale_log2e = HEAD_DIM ** -0.5 * LOG2E

    for i in range(blocks):
        rs = slice(i * BLOCK, (i + 1) * BLOCK)
        n = first_block + i
        ks = pl.multiple_of(jnp.clip((n - 1) * BLOCK, 0, seq - BAND), BLOCK)
        rel = (ks - n * BLOCK) + (lax.broadcasted_iota(jnp.int32, (BLOCK, BAND), 1)
                                  - lax.broadcasted_iota(jnp.int32, (BLOCK, BAND), 0))
        valid = jnp.abs(rel) <= WINDOW
        for g in range(n_kv):
            kv_sl = slice(g * HEAD_DIM, (g + 1) * HEAD_DIM)
            kb = k_ref[pl.ds(ks, BAND), kv_sl]
            vb = v_ref[pl.ds(ks, BAND), kv_sl]
            heads = range(g * GQA_GROUP, (g + 1) * GQA_GROUP)
            q4 = jnp.concatenate([q_ref[rs, hd * HEAD_DIM:(hd + 1) * HEAD_DIM] for hd in heads], axis=0)
            s4 = lax.dot_general(q4, kb, (((1,), (1,)), ((), ())), preferred_element_type=_F32)
            ps, denoms = [], []
            for j, hd in enumerate(heads):
                s = jnp.where(valid, s4[j * BLOCK:(j + 1) * BLOCK] * scale_log2e, MASK_VALUE)
                sk = sink_ref[layer, hd] * LOG2E
                m = jnp.maximum(jnp.max(s, axis=-1, keepdims=True), sk)
                p = jnp.exp2(s - m)
                denoms.append(jnp.sum(p, axis=-1, keepdims=True) + jnp.exp2(sk - m))
                ps.append(p.astype(_BF16))
            o4 = _dot(jnp.concatenate(ps, axis=0), vb)
            for j, hd in enumerate(heads):
                attn_scr[rs, hd * HEAD_DIM:(hd + 1) * HEAD_DIM] = o4[j * BLOCK:(j + 1) * BLOCK] / denoms[j]
        for hd in range(n_gmlp):
            sl = slice(hd * HEAD_DIM, (hd + 1) * HEAD_DIM)
            f = _dot(ws_ref[hd], vn_ref[rs, sl]) + bst_ref[:, hd:hd + 1]
            sgu_scr[rs, sl] = u_ref[rs, sl].astype(_F32) * f

    group = MIX_GROUP_BLOCKS * BLOCK
    for r0 in range(0, tm, group):
        rows = slice(r0, r0 + group)
        mix_scr[rows, :attn_w] = _rms(attn_scr[rows, :], ag_ref[...]).astype(_BF16)
        mix_scr[rows, attn_w:] = _rms(sgu_scr[rows, :], sg_ref[...]).astype(_BF16)
        x_new = x_ref[rows, :] + _dot(mix_scr[rows, :], wo_ref[...])
        xo_ref[rows, :] = x_new
        hn_ref[rows, :] = _rms(x_new, g2_ref[...]).astype(_BF16)


def _mixer(layer, sink, q, k, v, u, vn, x2, wo, ws, bst, ag, sg, g2, *, batch, seq):
    tokens, d_model = x2.shape
    attn_w, kv_w, gmlp_w = q.shape[1], k.shape[1], u.shape[1]
    tm = MIX_TM
    tiles = seq // tm
    row = lambda w: pl.BlockSpec((tm, w), lambda b, t: (b * tiles + t, 0))
    whole_seq = pl.BlockSpec((seq, kv_w), lambda b, t: (b, 0))
    lb = functools.partial(_layer_block, layer=layer)
    body = functools.partial(_mixer_body, layer=layer, seq=seq, n_kv=kv_w // HEAD_DIM,
                             n_gmlp=gmlp_w // HEAD_DIM)
    return pl.pallas_call(
        body,
        grid=(batch, tiles),
        in_specs=[pl.BlockSpec(memory_space=pltpu.SMEM), row(attn_w), whole_seq, whole_seq, row(gmlp_w),
                  row(gmlp_w), row(d_model), _resident(wo), lb(ws), lb(bst), lb(ag), lb(sg), lb(g2)],
        out_specs=[row(d_model), row(d_model)],
        out_shape=[jax.ShapeDtypeStruct((tokens, d_model), _F32),
                   jax.ShapeDtypeStruct((tokens, d_model), _BF16)],
        scratch_shapes=[pltpu.VMEM((tm, attn_w), _F32), pltpu.VMEM((tm, gmlp_w), _F32),
                        pltpu.VMEM((tm, d_model), _BF16)],
        compiler_params=pltpu.CompilerParams(dimension_semantics=("parallel", "parallel"),
                                             vmem_limit_bytes=VMEM_LIMIT_BYTES),
        name="mixer",
    )(sink, q, k, v, u, vn, x2, wo, ws, bst, ag, sg, g2)


def _ffn_body(prev_ref, hn_hbm, next_ref, x_hbm, wg_ref, wu_ref, cw_ref, cb_ref,
              wd_ref, o_ref, win_scr, ag_scr, au_scr, main_scr, x_scr, sems, *, tiles_per_seq, n_tiles):
    tm = main_scr.shape[0]
    m = pl.program_id(0)
    j = pl.program_id(1)

    def tile_copies(tile):
        rows = pl.ds(pl.multiple_of(tile * tm, tm), tm)
        return (pltpu.make_async_copy(hn_hbm.at[rows, :], main_scr, sems.at[0]),
                pltpu.make_async_copy(x_hbm.at[rows, :], x_scr, sems.at[1]))

    @pl.when(jnp.logical_and(m == 0, j == 0))
    def _():
        for copy in tile_copies(m):
            copy.start()

    @pl.when(j == 0)
    def _():
        for copy in tile_copies(m):
            copy.wait()
        t = m % tiles_per_seq
        win_scr[:HALO] = jnp.where(t == 0, jnp.zeros_like(prev_ref[...]), prev_ref[...])
        win_scr[HALO:HALO + tm] = main_scr[...]
        win_scr[HALO + tm:] = jnp.where(t == tiles_per_seq - 1, jnp.zeros_like(next_ref[...]), next_ref[...])
        o_ref[...] = x_scr[...]

    @pl.when(jnp.logical_and(j == 1, m + 1 < n_tiles))
    def _():
        for copy in tile_copies(m + 1):
            copy.start()

    win = win_scr[...]
    slabs = ag_scr.shape[0]

    def up_project(w_ref, a_scr):
        a = _dot(win, w_ref[...])
        for s in range(slabs):
            a_scr[s] = a[:, s * LANES:(s + 1) * LANES]

    tf = slabs * LANES
    d_ff = cw_ref.shape[1] // 2

    def conv(a_scr, col0, s):
        cols = pl.ds(pl.multiple_of(col0 + j * tf + s * LANES, LANES), LANES)
        below = a_scr[s, pl.ds(HALO - 1, tm, stride=1), :]
        mid = a_scr[s, pl.ds(HALO, tm), :]
        above = a_scr[s, pl.ds(HALO + 1, tm, stride=1), :]
        return cb_ref[:, cols] + below * cw_ref[0:1, cols] + mid * cw_ref[1:2, cols] + above * cw_ref[2:3, cols]

    up_project(wg_ref, ag_scr)
    up_project(wu_ref, au_scr)
    acts = []
    for s in range(slabs):
        gate = conv(ag_scr, 0, s)
        up = conv(au_scr, d_ff, s)
        acts.append((gate * (1.0 / (1.0 + jnp.exp(-gate))) * up).astype(_BF16))
    o_ref[...] += _dot(jnp.concatenate(acts, axis=1), wd_ref[...])


def _ffn(layer, hn, x2, w_up, conv_w, conv_b, w_down, *, seq):
    tokens, d_model = x2.shape
    d_ff = w_down.shape[0]
    tm, tf = FFN_TM, FFN_TF
    tiles_per_seq = seq // tm
    halo_blocks = tm // HALO
    last_halo = tokens // HALO - 1
    u_off = d_ff // tf
    wrows = tm + 2 * HALO
    n_tiles = tokens // tm
    assert d_ff // tf >= 2
    a_scratch = pltpu.VMEM((tf // LANES, wrows, LANES), _F32)
    return pl.pallas_call(
        functools.partial(_ffn_body, tiles_per_seq=tiles_per_seq, n_tiles=n_tiles),
        grid=(n_tiles, d_ff // tf),
        in_specs=[
            pl.BlockSpec((HALO, d_model), lambda m, j: (jnp.maximum(m * halo_blocks - 1, 0), 0)),
            pl.BlockSpec(memory_space=pl.ANY),
            pl.BlockSpec((HALO, d_model), lambda m, j: (jnp.minimum((m + 1) * halo_blocks, last_halo), 0)),
            pl.BlockSpec(memory_space=pl.ANY),
            pl.BlockSpec((d_model, tf), lambda m, j: (0, j)),
            pl.BlockSpec((d_model, tf), lambda m, j: (0, j + u_off)),
            _layer_block(conv_w, layer),
            _layer_block(conv_b, layer),
            pl.BlockSpec((tf, d_model), lambda m, j: (j, 0)),
        ],
        out_specs=pl.BlockSpec((tm, d_model), lambda m, j: (m, 0)),
        out_shape=jax.ShapeDtypeStruct((tokens, d_model), _F32),
        scratch_shapes=[pltpu.VMEM((wrows, d_model), _BF16), a_scratch, a_scratch,
                        pltpu.VMEM((tm, d_model), _BF16), pltpu.VMEM((tm, d_model), _F32),
                        pltpu.SemaphoreType.DMA((2,))],
        compiler_params=pltpu.CompilerParams(dimension_semantics=("arbitrary", "arbitrary"),
                                             vmem_limit_bytes=VMEM_LIMIT_BYTES),
        name="ffn",
    )(hn, hn, hn, x2, w_up, w_up, conv_w, conv_b, w_down)


def _rope_tables(seq):
    inv_freq = ROPE_THETA ** (-jnp.arange(0, HEAD_DIM, 2, dtype=_F32) / HEAD_DIM)
    ang = jnp.arange(seq, dtype=_F32)[:, None] * inv_freq[None, :]
    cos, sin = jnp.cos(ang), jnp.sin(ang)
    return jnp.concatenate([cos, cos], axis=-1), jnp.concatenate([-sin, sin], axis=-1)


def kernel(x, norm1_g, w_in, q_norm_g, k_norm_g, sink, sgu_ln_g, sgu_ln_b, w_s, b_s,
           attn_out_g, sgu_out_g, w_o, norm2_g, w_up, conv_w, conv_b, w_down):
    batch, seq, d_model = x.shape
    depth = w_in.shape[0]
    attn_w = attn_out_g.shape[1]
    gmlp_w = sgu_out_g.shape[1]
    kv_w = (w_in.shape[2] - attn_w - 2 * gmlp_w) // 2
    assert seq % max(IN_TM, MIX_TM, FFN_TM) == 0 and w_down.shape[1] % FFN_TF == 0
    assert w_s.shape[2] == BLOCK and seq >= BAND and MIX_TM % (MIX_GROUP_BLOCKS * BLOCK) == 0

    cosf, sinf = _rope_tables(seq)
    rows = lambda a: a.reshape(depth, 1, -1)
    g1, qg, kg, lng, lnb = (rows(a) for a in (norm1_g, q_norm_g, k_norm_g, sgu_ln_g, sgu_ln_b))
    ag, sg, g2, cb = (rows(a) for a in (attn_out_g, sgu_out_g, norm2_g, conv_b))
    bst = jnp.swapaxes(b_s, 1, 2)
    ws = w_s.astype(_BF16)
    win = w_in[0].astype(_BF16)

    x2 = x.reshape(batch * seq, d_model)
    for l in range(depth):
        to_cast = [(w_o, l), (w_up, l), (w_down, l)] + ([(w_in, l + 1)] if l + 1 < depth else [])
        q, k, v, u, vn, wo, wup, wdn, *nxt = _in_proj(l, x2, g1, win, qg, kg, cosf, sinf, lng, lnb, to_cast,
                                                      seq=seq, attn_w=attn_w, kv_w=kv_w, gmlp_w=gmlp_w)
        win = nxt[0] if nxt else None
        x2, hn = _mixer(l, sink, q, k, v, u, vn, x2, wo, ws, bst, ag, sg, g2, batch=batch, seq=seq)
        x2 = _ffn(l, hn, x2, wup, conv_w, cb, wdn, seq=seq)
    return x2.reshape(batch, seq, d_model)
```

```python
import functools

import jax
import jax.numpy as jnp
import numpy as np
from jax import lax
from jax.experimental import pallas as pl
from jax.experimental.pallas import tpu as pltpu

HEAD_DIM = 128
GQA_GROUP = 4
WINDOW = 128
BLOCK = 128
BAND = 3 * BLOCK
ROPE_THETA = 10000.0
EPS = 1e-6
MASK_VALUE = -1e30
LOG2E = 1.4426950408889634

BF16_TILE_ROWS = 16
LANES = 128
HALO = BF16_TILE_ROWS
VMEM_LIMIT_BYTES = 56 * 1024 * 1024

IN_TM = 512
MIX_TM = 512
MIX_GROUP_BLOCKS = 2
FFN_TM = 1024
FFN_TF = 512

_BF16 = jnp.bfloat16
_F32 = jnp.float32


def _dot(a, b):
    return jnp.dot(a, b, preferred_element_type=_F32)


def _rms(x, g):
    return x * lax.rsqrt(jnp.mean(x * x, axis=-1, keepdims=True) + EPS) * g


def _gelu_tanh(x):
    c = np.float32(np.sqrt(2.0 / np.pi))
    return x * (0.5 * (1.0 + jnp.tanh(c * (x + np.float32(0.044715) * (x * x * x)))))


def _resident(arr):
    zeros = (0,) * arr.ndim
    return pl.BlockSpec(arr.shape, lambda *_: zeros, pipeline_mode=pl.Buffered(1))


def _layer_block(arr, layer):
    tail = (0,) * (arr.ndim - 1)
    return pl.BlockSpec((None,) + arr.shape[1:], lambda *_: (layer,) + tail, pipeline_mode=pl.Buffered(1))


def _in_proj_body(x_ref, g1_ref, win_ref, qg_ref, kg_ref, cos_ref, sin_ref, lng_ref, lnb_ref, *refs,
                  layer, n_cast, attn_w, kv_w, gmlp_w):
    row = lambda ref: ref[layer:layer + 1, :]
    cast_in = refs[:n_cast]
    q_ref, k_ref, v_ref, u_ref, vn_ref = refs[n_cast:n_cast + 5]
    cast_out = refs[n_cast + 5:]
    for src, dst in zip(cast_in, cast_out):
        dst[...] = src[...].astype(_BF16)
    h = _rms(x_ref[...], row(g1_ref)).astype(_BF16)
    cosf = cos_ref[...]
    sinf = sin_ref[...]

    def qk_heads(col0, n_heads, gain, out_ref):
        y = _dot(h, win_ref[:, col0:col0 + n_heads * HEAD_DIM])
        for hd in range(n_heads):
            sl = slice(hd * HEAD_DIM, (hd + 1) * HEAD_DIM)
            t = _rms(y[:, sl], gain)
            out_ref[:, sl] = (t * cosf + pltpu.roll(t, HEAD_DIM // 2, 1) * sinf).astype(_BF16)

    c_v = attn_w + kv_w
    c_u = c_v + kv_w
    c_gv = c_u + gmlp_w
    qk_heads(0, attn_w // HEAD_DIM, row(qg_ref), q_ref)
    qk_heads(attn_w, kv_w // HEAD_DIM, row(kg_ref), k_ref)
    v_ref[...] = _dot(h, win_ref[:, c_v:c_v + kv_w]).astype(_BF16)
    gv = _gelu_tanh(_dot(h, win_ref[:, c_gv:c_gv + gmlp_w]))
    xc = gv - jnp.mean(gv, axis=-1, keepdims=True)
    vn = xc * lax.rsqrt(jnp.mean(xc * xc, axis=-1, keepdims=True) + EPS) * row(lng_ref) + row(lnb_ref)
    vn_ref[...] = vn.astype(_BF16)
    half = gmlp_w // 2
    for c0 in (0, half):
        u_ref[:, c0:c0 + half] = _gelu_tanh(_dot(h, win_ref[:, c_u + c0:c_u + c0 + half])).astype(_BF16)


def _in_proj(layer, x2, g1, win, qg, kg, cosf, sinf, lng, lnb, to_cast, *, seq, attn_w, kv_w, gmlp_w):
    tokens, d_model = x2.shape
    tm = IN_TM
    steps = tokens // tm
    tiles_per_seq = seq // tm
    row = lambda w: pl.BlockSpec((tm, w), lambda m: (m, 0))
    pos = pl.BlockSpec((tm, HEAD_DIM), lambda m: (m % tiles_per_seq, 0))
    cast_in, cast_out, cast_shapes = [], [], []
    for w, wl in to_cast:
        assert w.shape[1] % (steps * BF16_TILE_ROWS) == 0
        slab = (w.shape[1] // steps, w.shape[2])
        cast_in.append(pl.BlockSpec((None,) + slab, lambda m, wl=wl: (wl, m, 0)))
        cast_out.append(pl.BlockSpec(slab, lambda m: (m, 0)))
        cast_shapes.append(jax.ShapeDtypeStruct(w.shape[1:], _BF16))
    return pl.pallas_call(
        functools.partial(_in_proj_body, layer=layer, n_cast=len(to_cast), attn_w=attn_w, kv_w=kv_w, gmlp_w=gmlp_w),
        grid=(steps,),
        in_specs=[row(d_model), _resident(g1), _resident(win), _resident(qg), _resident(kg), pos, pos,
                  _resident(lng), _resident(lnb)] + cast_in,
        out_specs=[row(attn_w), row(kv_w), row(kv_w), row(gmlp_w), row(gmlp_w)] + cast_out,
        out_shape=[jax.ShapeDtypeStruct((tokens, w), _BF16) for w in (attn_w, kv_w, kv_w, gmlp_w, gmlp_w)]
        + cast_shapes,
        compiler_params=pltpu.CompilerParams(dimension_semantics=("parallel",),
                                             vmem_limit_bytes=VMEM_LIMIT_BYTES),
        name="in_proj",
    )(x2, g1, win, qg, kg, cosf, sinf, lng, lnb, *(w for w, _ in to_cast))


def _mixer_body(sink_ref, q_ref, k_ref, v_ref, u_ref, vn_ref, x_ref, wo_ref, ws_ref, bst_ref,
                ag_ref, sg_ref, g2_ref, xo_ref, hn_ref, attn_scr, sgu_scr, mix_scr,
                *, layer, seq, n_kv, n_gmlp):
    tm = q_ref.shape[0]
    attn_w = q_ref.shape[1]
    blocks = tm // BLOCK
    first_block = pl.program_id(1) * blocks
    scale_log2e = HEAD_DIM ** -0.5 * LOG2E
    row = lambda ref: ref[layer:layer + 1, :]

    for i in range(blocks):
        rs = slice(i * BLOCK, (i + 1) * BLOCK)
        n = first_block + i
        ks = pl.multiple_of(jnp.clip((n - 1) * BLOCK, 0, seq - BAND), BLOCK)
        rel = (ks - n * BLOCK) + (lax.broadcasted_iota(jnp.int32, (BLOCK, BAND), 1)
                                  - lax.broadcasted_iota(jnp.int32, (BLOCK, BAND), 0))
        valid = jnp.abs(rel) <= WINDOW
        for g in range(n_kv):
            kv_sl = slice(g * HEAD_DIM, (g + 1) * HEAD_DIM)
            kb = k_ref[pl.ds(ks, BAND), kv_sl]
            vb = v_ref[pl.ds(ks, BAND), kv_sl]
            heads = range(g * GQA_GROUP, (g + 1) * GQA_GROUP)
            q4 = jnp.concatenate([q_ref[rs, hd * HEAD_DIM:(hd + 1) * HEAD_DIM] for hd in heads], axis=0)
            s4 = lax.dot_general(q4, kb, (((1,), (1,)), ((), ())), preferred_element_type=_F32)
            ps, denoms = [], []
            for j, hd in enumerate(heads):
                s = jnp.where(valid, s4[j * BLOCK:(j + 1) * BLOCK] * scale_log2e, MASK_VALUE)
                sk = sink_ref[layer, hd] * LOG2E
                m = jnp.maximum(jnp.max(s, axis=-1, keepdims=True), sk)
                p = jnp.exp2(s - m)
                denoms.append(jnp.sum(p, axis=-1, keepdims=True) + jnp.exp2(sk - m))
                ps.append(p.astype(_BF16))
            o4 = _dot(jnp.concatenate(ps, axis=0), vb)
            for j, hd in enumerate(heads):
                attn_scr[rs, hd * HEAD_DIM:(hd + 1) * HEAD_DIM] = o4[j * BLOCK:(j + 1) * BLOCK] / denoms[j]

    for i in range(blocks):
        rs = slice(i * BLOCK, (i + 1) * BLOCK)
        for hd in range(n_gmlp):
            sl = slice(hd * HEAD_DIM, (hd + 1) * HEAD_DIM)
            f = _dot(ws_ref[hd], vn_ref[rs, sl]) + bst_ref[:, hd:hd + 1]
            sgu_scr[rs, sl] = u_ref[rs, sl].astype(_F32) * f

    group = MIX_GROUP_BLOCKS * BLOCK
    for r0 in range(0, tm, group):
        rows = slice(r0, r0 + group)
        mix_scr[rows, :attn_w] = _rms(attn_scr[rows, :], row(ag_ref)).astype(_BF16)
        mix_scr[rows, attn_w:] = _rms(sgu_scr[rows, :], row(sg_ref)).astype(_BF16)
        x_new = x_ref[rows, :] + _dot(mix_scr[rows, :], wo_ref[...])
        xo_ref[rows, :] = x_new
        hn_ref[rows, :] = _rms(x_new, row(g2_ref)).astype(_BF16)


def _mixer(layer, sink, q, k, v, u, vn, x2, wo, ws, bst, ag, sg, g2, *, batch, seq):
    tokens, d_model = x2.shape
    attn_w, kv_w, gmlp_w = q.shape[1], k.shape[1], u.shape[1]
    tm = MIX_TM
    tiles = seq // tm
    row = lambda w: pl.BlockSpec((tm, w), lambda b, t: (b * tiles + t, 0))
    whole_seq = pl.BlockSpec((seq, kv_w), lambda b, t: (b, 0))
    lb = functools.partial(_layer_block, layer=layer)
    body = functools.partial(_mixer_body, layer=layer, seq=seq, n_kv=kv_w // HEAD_DIM,
                             n_gmlp=gmlp_w // HEAD_DIM)
    return pl.pallas_call(
        body,
        grid=(batch, tiles),
        in_specs=[pl.BlockSpec(memory_space=pltpu.SMEM), row(attn_w), whole_seq, whole_seq, row(gmlp_w),
                  row(gmlp_w), row(d_model), _resident(wo), lb(ws), lb(bst), _resident(ag), _resident(sg),
                  _resident(g2)],
        out_specs=[row(d_model), row(d_model)],
        out_shape=[jax.ShapeDtypeStruct((tokens, d_model), _F32),
                   jax.ShapeDtypeStruct((tokens, d_model), _BF16)],
        scratch_shapes=[pltpu.VMEM((tm, attn_w), _F32), pltpu.VMEM((tm, gmlp_w), _F32),
                        pltpu.VMEM((tm, d_model), _BF16)],
        compiler_params=pltpu.CompilerParams(dimension_semantics=("parallel", "parallel"),
                                             vmem_limit_bytes=VMEM_LIMIT_BYTES),
        name="mixer",
    )(sink, q, k, v, u, vn, x2, wo, ws, bst, ag, sg, g2)


def _ffn_body(prev_ref, hn_hbm, next_ref, x_hbm, wg_ref, wu_ref, cw_ref, cb_ref,
              wd_ref, o_ref, win_scr, ag_scr, au_scr, main_scr, x_scr, sems, *, layer, tiles_per_seq, n_tiles):
    tm = main_scr.shape[0]
    m = pl.program_id(0)
    j = pl.program_id(1)

    def tile_copies(tile):
        rows = pl.ds(pl.multiple_of(tile * tm, tm), tm)
        return (pltpu.make_async_copy(hn_hbm.at[rows, :], main_scr, sems.at[0]),
                pltpu.make_async_copy(x_hbm.at[rows, :], x_scr, sems.at[1]))

    @pl.when(jnp.logical_and(m == 0, j == 0))
    def _():
        for copy in tile_copies(m):
            copy.start()

    @pl.when(j == 0)
    def _():
        for copy in tile_copies(m):
            copy.wait()
        t = m % tiles_per_seq
        win_scr[:HALO] = jnp.where(t == 0, jnp.zeros_like(prev_ref[...]), prev_ref[...])
        win_scr[HALO:HALO + tm] = main_scr[...]
        win_scr[HALO + tm:] = jnp.where(t == tiles_per_seq - 1, jnp.zeros_like(next_ref[...]), next_ref[...])
        o_ref[...] = x_scr[...]

    @pl.when(jnp.logical_and(j == 1, m + 1 < n_tiles))
    def _():
        for copy in tile_copies(m + 1):
            copy.start()

    win = win_scr[...]
    slabs = ag_scr.shape[0]

    def up_project(w_ref, a_scr):
        a = _dot(win, w_ref[...])
        for s in range(slabs):
            a_scr[s] = a[:, s * LANES:(s + 1) * LANES]

    tf = slabs * LANES
    d_ff = cw_ref.shape[1] // 2

    def conv(a_scr, col0, s):
        cols = pl.ds(pl.multiple_of(col0 + j * tf + s * LANES, LANES), LANES)
        below = a_scr[s, pl.ds(HALO - 1, tm, stride=1), :]
        mid = a_scr[s, pl.ds(HALO, tm), :]
        above = a_scr[s, pl.ds(HALO + 1, tm, stride=1), :]
        return (cb_ref[layer:layer + 1, cols] + below * cw_ref[0:1, cols] + mid * cw_ref[1:2, cols]
                + above * cw_ref[2:3, cols])

    up_project(wg_ref, ag_scr)
    up_project(wu_ref, au_scr)
    acts = []
    for s in range(slabs):
        gate = conv(ag_scr, 0, s)
        up = conv(au_scr, d_ff, s)
        acts.append((gate * (1.0 / (1.0 + jnp.exp(-gate))) * up).astype(_BF16))
    o_ref[...] += _dot(jnp.concatenate(acts, axis=1), wd_ref[...])


def _ffn(layer, hn, x2, w_up, conv_w, conv_b, w_down, *, seq):
    tokens, d_model = x2.shape
    d_ff = w_down.shape[0]
    tm, tf = FFN_TM, FFN_TF
    tiles_per_seq = seq // tm
    halo_blocks = tm // HALO
    last_halo = tokens // HALO - 1
    u_off = d_ff // tf
    wrows = tm + 2 * HALO
    n_tiles = tokens // tm
    assert d_ff // tf >= 2
    buffers = (3 * tm * d_model * 4 + tm * d_model * 2 + wrows * d_model * 2 + 2 * wrows * tf * 4
               + 2 * 3 * d_model * tf * 2)
    assert buffers < VMEM_LIMIT_BYTES, (buffers, VMEM_LIMIT_BYTES)
    a_scratch = pltpu.VMEM((tf // LANES, wrows, LANES), _F32)
    return pl.pallas_call(
        functools.partial(_ffn_body, layer=layer, tiles_per_seq=tiles_per_seq, n_tiles=n_tiles),
        grid=(n_tiles, d_ff // tf),
        in_specs=[
            pl.BlockSpec((HALO, d_model), lambda m, j: (jnp.maximum(m * halo_blocks - 1, 0), 0)),
            pl.BlockSpec(memory_space=pl.ANY),
            pl.BlockSpec((HALO, d_model), lambda m, j: (jnp.minimum((m + 1) * halo_blocks, last_halo), 0)),
            pl.BlockSpec(memory_space=pl.ANY),
            pl.BlockSpec((d_model, tf), lambda m, j: (0, j)),
            pl.BlockSpec((d_model, tf), lambda m, j: (0, j + u_off)),
            _layer_block(conv_w, layer),
            _resident(conv_b),
            pl.BlockSpec((tf, d_model), lambda m, j: (j, 0)),
        ],
        out_specs=pl.BlockSpec((tm, d_model), lambda m, j: (m, 0)),
        out_shape=jax.ShapeDtypeStruct((tokens, d_model), _F32),
        scratch_shapes=[pltpu.VMEM((wrows, d_model), _BF16), a_scratch, a_scratch,
                        pltpu.VMEM((tm, d_model), _BF16), pltpu.VMEM((tm, d_model), _F32),
                        pltpu.SemaphoreType.DMA((2,))],
        compiler_params=pltpu.CompilerParams(dimension_semantics=("arbitrary", "arbitrary"),
                                             vmem_limit_bytes=VMEM_LIMIT_BYTES),
        name="ffn",
    )(hn, hn, hn, x2, w_up, w_up, conv_w, conv_b, w_down)


def _rope_tables(seq):
    inv_freq = ROPE_THETA ** (-np.arange(0, HEAD_DIM, 2, dtype=np.float64) / HEAD_DIM)
    ang = np.arange(seq, dtype=np.float64)[:, None] * inv_freq[None, :]
    cos, sin = np.cos(ang), np.sin(ang)
    table = lambda a, b: jnp.asarray(np.concatenate([a, b], axis=-1).astype(np.float32))
    return table(cos, cos), table(-sin, sin)


def kernel(x, norm1_g, w_in, q_norm_g, k_norm_g, sink, sgu_ln_g, sgu_ln_b, w_s, b_s,
           attn_out_g, sgu_out_g, w_o, norm2_g, w_up, conv_w, conv_b, w_down):
    batch, seq, d_model = x.shape
    depth = w_in.shape[0]
    attn_w = attn_out_g.shape[1]
    gmlp_w = sgu_out_g.shape[1]
    kv_w = (w_in.shape[2] - attn_w - 2 * gmlp_w) // 2
    assert seq % max(IN_TM, MIX_TM, FFN_TM) == 0 and w_down.shape[1] % FFN_TF == 0
    assert w_s.shape[2] == BLOCK and seq >= BAND and MIX_TM % (MIX_GROUP_BLOCKS * BLOCK) == 0

    cosf, sinf = _rope_tables(seq)
    g1, qg, kg, lng, lnb = norm1_g, q_norm_g, k_norm_g, sgu_ln_g, sgu_ln_b
    ag, sg, g2, cb = attn_out_g, sgu_out_g, norm2_g, conv_b
    bst = jnp.swapaxes(b_s, 1, 2)
    ws = w_s.astype(_BF16)
    win = w_in[0].astype(_BF16)

    x2 = x.reshape(batch * seq, d_model)
    for l in range(depth):
        to_cast = [(w_o, l), (w_up, l), (w_down, l)] + ([(w_in, l + 1)] if l + 1 < depth else [])
        q, k, v, u, vn, wo, wup, wdn, *nxt = _in_proj(l, x2, g1, win, qg, kg, cosf, sinf, lng, lnb, to_cast,
                                                      seq=seq, attn_w=attn_w, kv_w=kv_w, gmlp_w=gmlp_w)
        win = nxt[0] if nxt else None
        x2, hn = _mixer(l, sink, q, k, v, u, vn, x2, wo, ws, bst, ag, sg, g2, batch=batch, seq=seq)
        x2 = _ffn(l, hn, x2, wup, conv_w, cb, wdn, seq=seq)
    return x2.reshape(batch, seq, d_model)
```

```python
import functools

import jax
import jax.numpy as jnp
import numpy as np
from jax import lax
from jax.experimental import pallas as pl
from jax.experimental.pallas import tpu as pltpu

HEAD_DIM = 128
GQA_GROUP = 4
WINDOW = 128
BLOCK = 128
BAND = 3 * BLOCK
ROPE_THETA = 10000.0
EPS = 1e-6
MASK_VALUE = -1e30
LOG2E = 1.4426950408889634

BF16_TILE_ROWS = 16
LANES = 128
HALO = BF16_TILE_ROWS
VMEM_LIMIT_BYTES = 56 * 1024 * 1024

IN_TM = 512
MIX_TM = 512
MIX_GROUP_BLOCKS = 2
FFN_TM = 1024
FFN_TF = 512

_BF16 = jnp.bfloat16
_F32 = jnp.float32


def _dot(a, b):
    return jnp.dot(a, b, preferred_element_type=_F32)


def _rms(x, g):
    return x * lax.rsqrt(jnp.mean(x * x, axis=-1, keepdims=True) + EPS) * g


def _gelu_tanh(x):
    c = np.float32(np.sqrt(2.0 / np.pi))
    return x * (0.5 * (1.0 + jnp.tanh(c * (x + np.float32(0.044715) * (x * x * x)))))


def _resident(arr):
    zeros = (0,) * arr.ndim
    return pl.BlockSpec(arr.shape, lambda *_: zeros, pipeline_mode=pl.Buffered(1))


def _layer_block(arr, layer):
    tail = (0,) * (arr.ndim - 1)
    return pl.BlockSpec((None,) + arr.shape[1:], lambda *_: (layer,) + tail, pipeline_mode=pl.Buffered(1))


def _in_proj_body(x_ref, g1_ref, win_ref, qg_ref, kg_ref, cos_ref, sin_ref, lng_ref, lnb_ref, *refs,
                  layer, n_cast, attn_w, kv_w, gmlp_w):
    row = lambda ref: ref[layer:layer + 1, :]
    cast_in = refs[:n_cast]
    q_ref, k_ref, v_ref, u_ref, vn_ref = refs[n_cast:n_cast + 5]
    cast_out = refs[n_cast + 5:]
    for src, dst in zip(cast_in, cast_out):
        dst[...] = src[...].astype(_BF16)
    h = _rms(x_ref[...], row(g1_ref)).astype(_BF16)
    cosf = cos_ref[...]
    sinf = sin_ref[...]

    def qk_heads(col0, n_heads, gain, out_ref):
        y = _dot(h, win_ref[:, col0:col0 + n_heads * HEAD_DIM])
        for hd in range(n_heads):
            sl = slice(hd * HEAD_DIM, (hd + 1) * HEAD_DIM)
            t = _rms(y[:, sl], gain)
            out_ref[:, sl] = (t * cosf + pltpu.roll(t, HEAD_DIM // 2, 1) * sinf).astype(_BF16)

    c_v = attn_w + kv_w
    c_u = c_v + kv_w
    c_gv = c_u + gmlp_w
    qk_heads(0, attn_w // HEAD_DIM, row(qg_ref), q_ref)
    qk_heads(attn_w, kv_w // HEAD_DIM, row(kg_ref), k_ref)
    v_ref[...] = _dot(h, win_ref[:, c_v:c_v + kv_w]).astype(_BF16)
    gv = _gelu_tanh(_dot(h, win_ref[:, c_gv:c_gv + gmlp_w]))
    xc = gv - jnp.mean(gv, axis=-1, keepdims=True)
    vn = xc * lax.rsqrt(jnp.mean(xc * xc, axis=-1, keepdims=True) + EPS) * row(lng_ref) + row(lnb_ref)
    vn_ref[...] = vn.astype(_BF16)
    half = gmlp_w // 2
    for c0 in (0, half):
        u_ref[:, c0:c0 + half] = _gelu_tanh(_dot(h, win_ref[:, c_u + c0:c_u + c0 + half])).astype(_BF16)


def _in_proj(layer, x2, g1, win, qg, kg, cosf, sinf, lng, lnb, to_cast, *, seq, attn_w, kv_w, gmlp_w):
    tokens, d_model = x2.shape
    tm = IN_TM
    steps = tokens // tm
    tiles_per_seq = seq // tm
    row = lambda w: pl.BlockSpec((tm, w), lambda m: (m, 0))
    pos = pl.BlockSpec((tm, HEAD_DIM), lambda m: (m % tiles_per_seq, 0))
    cast_in, cast_out, cast_shapes = [], [], []
    for w, wl in to_cast:
        assert w.shape[1] % (steps * BF16_TILE_ROWS) == 0
        slab = (w.shape[1] // steps, w.shape[2])
        cast_in.append(pl.BlockSpec((None,) + slab, lambda m, wl=wl: (wl, m, 0)))
        cast_out.append(pl.BlockSpec(slab, lambda m: (m, 0)))
        cast_shapes.append(jax.ShapeDtypeStruct(w.shape[1:], _BF16))
    return pl.pallas_call(
        functools.partial(_in_proj_body, layer=layer, n_cast=len(to_cast), attn_w=attn_w, kv_w=kv_w, gmlp_w=gmlp_w),
        grid=(steps,),
        in_specs=[row(d_model), _resident(g1), _resident(win), _resident(qg), _resident(kg), pos, pos,
                  _resident(lng), _resident(lnb)] + cast_in,
        out_specs=[row(attn_w), row(kv_w), row(kv_w), row(gmlp_w), row(gmlp_w)] + cast_out,
        out_shape=[jax.ShapeDtypeStruct((tokens, w), _BF16) for w in (attn_w, kv_w, kv_w, gmlp_w, gmlp_w)]
        + cast_shapes,
        compiler_params=pltpu.CompilerParams(dimension_semantics=("parallel",),
                                             vmem_limit_bytes=VMEM_LIMIT_BYTES),
        name="in_proj",
    )(x2, g1, win, qg, kg, cosf, sinf, lng, lnb, *(w for w, _ in to_cast))


def _mixer_body(sink_ref, q_ref, k_ref, v_ref, u_ref, vn_ref, x_ref, wo_ref, ws_ref, bst_ref,
                ag_ref, sg_ref, g2_ref, xo_ref, hn_ref, attn_scr, sgu_scr, mix_scr,
                *, layer, seq, n_kv, n_gmlp):
    tm = q_ref.shape[0]
    attn_w = q_ref.shape[1]
    blocks = tm // BLOCK
    first_block = pl.program_id(1) * blocks
    scale_log2e = HEAD_DIM ** -0.5 * LOG2E
    row = lambda ref: ref[layer:layer + 1, :]

    for i in range(blocks):
        rs = slice(i * BLOCK, (i + 1) * BLOCK)
        n = first_block + i
        ks = pl.multiple_of(jnp.clip((n - 1) * BLOCK, 0, seq - BAND), BLOCK)
        rel = (ks - n * BLOCK) + (lax.broadcasted_iota(jnp.int32, (BLOCK, BAND), 1)
                                  - lax.broadcasted_iota(jnp.int32, (BLOCK, BAND), 0))
        valid = jnp.abs(rel) <= WINDOW
        for g in range(n_kv):
            kv_sl = slice(g * HEAD_DIM, (g + 1) * HEAD_DIM)
            kb = k_ref[pl.ds(ks, BAND), kv_sl]
            vb = v_ref[pl.ds(ks, BAND), kv_sl]
            heads = range(g * GQA_GROUP, (g + 1) * GQA_GROUP)
            q4 = jnp.concatenate([q_ref[rs, hd * HEAD_DIM:(hd + 1) * HEAD_DIM] for hd in heads], axis=0)
            s4 = lax.dot_general(q4, kb, (((1,), (1,)), ((), ())), preferred_element_type=_F32)
            ps, denoms = [], []
            for j, hd in enumerate(heads):
                s = jnp.where(valid, s4[j * BLOCK:(j + 1) * BLOCK] * scale_log2e, MASK_VALUE)
                sk = sink_ref[layer, hd] * LOG2E
                m = jnp.maximum(jnp.max(s, axis=-1, keepdims=True), sk)
                p = jnp.exp2(s - m)
                denoms.append(jnp.sum(p, axis=-1, keepdims=True) + jnp.exp2(sk - m))
                ps.append(p.astype(_BF16))
            o4 = _dot(jnp.concatenate(ps, axis=0), vb)
            for j, hd in enumerate(heads):
                attn_scr[rs, hd * HEAD_DIM:(hd + 1) * HEAD_DIM] = o4[j * BLOCK:(j + 1) * BLOCK] / denoms[j]

    for i in range(blocks):
        rs = slice(i * BLOCK, (i + 1) * BLOCK)
        for hd in range(n_gmlp):
            sl = slice(hd * HEAD_DIM, (hd + 1) * HEAD_DIM)
            f = _dot(ws_ref[hd], vn_ref[rs, sl]) + bst_ref[:, hd:hd + 1]
            sgu_scr[rs, sl] = u_ref[rs, sl].astype(_F32) * f

    group = MIX_GROUP_BLOCKS * BLOCK
    for r0 in range(0, tm, group):
        rows = slice(r0, r0 + group)
        mix_scr[rows, :attn_w] = _rms(attn_scr[rows, :], row(ag_ref)).astype(_BF16)
        mix_scr[rows, attn_w:] = _rms(sgu_scr[rows, :], row(sg_ref)).astype(_BF16)
        x_new = x_ref[rows, :] + _dot(mix_scr[rows, :], wo_ref[...])
        xo_ref[rows, :] = x_new
        hn_ref[rows, :] = _rms(x_new, row(g2_ref)).astype(_BF16)


def _mixer(layer, sink, q, k, v, u, vn, x2, wo, ws, bst, ag, sg, g2, *, batch, seq):
    tokens, d_model = x2.shape
    attn_w, kv_w, gmlp_w = q.shape[1], k.shape[1], u.shape[1]
    tm = MIX_TM
    tiles = seq // tm
    row = lambda w: pl.BlockSpec((tm, w), lambda b, t: (b * tiles + t, 0))
    whole_seq = pl.BlockSpec((seq, kv_w), lambda b, t: (b, 0))
    lb = functools.partial(_layer_block, layer=layer)
    body = functools.partial(_mixer_body, layer=layer, seq=seq, n_kv=kv_w // HEAD_DIM,
                             n_gmlp=gmlp_w // HEAD_DIM)
    return pl.pallas_call(
        body,
        grid=(batch, tiles),
        in_specs=[pl.BlockSpec(memory_space=pltpu.SMEM), row(attn_w), whole_seq, whole_seq, row(gmlp_w),
                  row(gmlp_w), row(d_model), _resident(wo), lb(ws), lb(bst), _resident(ag), _resident(sg),
                  _resident(g2)],
        out_specs=[row(d_model), row(d_model)],
        out_shape=[jax.ShapeDtypeStruct((tokens, d_model), _F32),
                   jax.ShapeDtypeStruct((tokens, d_model), _BF16)],
        scratch_shapes=[pltpu.VMEM((tm, attn_w), _F32), pltpu.VMEM((tm, gmlp_w), _F32),
                        pltpu.VMEM((tm, d_model), _BF16)],
        compiler_params=pltpu.CompilerParams(dimension_semantics=("parallel", "parallel"),
                                             vmem_limit_bytes=VMEM_LIMIT_BYTES),
        name="mixer",
    )(sink, q, k, v, u, vn, x2, wo, ws, bst, ag, sg, g2)


def _ffn_body(prev_ref, hn_hbm, next_ref, x_hbm, wup_hbm, cw_ref, cb_ref, wdn_hbm,
              o_ref, win_scr, ag_scr, au_scr, main_scr, x_scr, wg_buf, wu_buf, wd_buf, tile_sems, w_sems,
              *, layer, tiles_per_seq, n_tiles, nj):
    tm = main_scr.shape[0]
    tf = wg_buf.shape[2]
    d_ff = nj * tf
    slabs = ag_scr.shape[0]
    m = pl.program_id(0)

    def tile_copies(tile):
        rows = pl.ds(pl.multiple_of(tile * tm, tm), tm)
        return (pltpu.make_async_copy(hn_hbm.at[rows, :], main_scr, tile_sems.at[0]),
                pltpu.make_async_copy(x_hbm.at[rows, :], x_scr, tile_sems.at[1]))

    def weight_copies(jj, slot):
        c0 = pl.multiple_of(jj * tf, tf)
        return (pltpu.make_async_copy(wup_hbm.at[:, pl.ds(c0, tf)], wg_buf.at[slot], w_sems.at[0, slot]),
                pltpu.make_async_copy(wup_hbm.at[:, pl.ds(d_ff + c0, tf)], wu_buf.at[slot], w_sems.at[1, slot]),
                pltpu.make_async_copy(wdn_hbm.at[pl.ds(c0, tf), :], wd_buf.at[slot], w_sems.at[2, slot]))

    @pl.when(m == 0)
    def _():
        for copy in tile_copies(m) + weight_copies(0, 0):
            copy.start()

    for copy in tile_copies(m):
        copy.wait()
    t = m % tiles_per_seq
    win_scr[:HALO] = jnp.where(t == 0, jnp.zeros_like(prev_ref[...]), prev_ref[...])
    win_scr[HALO:HALO + tm] = main_scr[...]
    win_scr[HALO + tm:] = jnp.where(t == tiles_per_seq - 1, jnp.zeros_like(next_ref[...]), next_ref[...])
    o_ref[...] = x_scr[...]

    @pl.when(m + 1 < n_tiles)
    def _():
        for copy in tile_copies(m + 1):
            copy.start()

    def step(j, carry):
        g = m * nj + j
        slot = lax.rem(g, 2)
        for copy in weight_copies(j, slot):
            copy.wait()

        @pl.when(g + 1 < n_tiles * nj)
        def _():
            for copy in weight_copies(jnp.where(j + 1 == nj, 0, j + 1), 1 - slot):
                copy.start()

        win = win_scr[...]
        for w_buf, a_scr in ((wg_buf, ag_scr), (wu_buf, au_scr)):
            a = _dot(win, w_buf[slot])
            for s in range(slabs):
                a_scr[s] = a[:, s * LANES:(s + 1) * LANES]

        def conv(a_scr, col0, s):
            cols = pl.ds(pl.multiple_of(col0 + j * tf + s * LANES, LANES), LANES)
            below = a_scr[s, pl.ds(HALO - 1, tm, stride=1), :]
            mid = a_scr[s, pl.ds(HALO, tm), :]
            above = a_scr[s, pl.ds(HALO + 1, tm, stride=1), :]
            return (cb_ref[layer:layer + 1, cols] + below * cw_ref[0:1, cols] + mid * cw_ref[1:2, cols]
                    + above * cw_ref[2:3, cols])

        acts = []
        for s in range(slabs):
            gate = conv(ag_scr, 0, s)
            up = conv(au_scr, d_ff, s)
            acts.append((gate * (1.0 / (1.0 + jnp.exp(-gate))) * up).astype(_BF16))
        o_ref[...] += _dot(jnp.concatenate(acts, axis=1), wd_buf[slot])
        return carry

    lax.fori_loop(0, nj, step, 0)


def _ffn(layer, hn, x2, w_up, conv_w, conv_b, w_down, *, seq):
    tokens, d_model = x2.shape
    d_ff = w_down.shape[0]
    tm, tf = FFN_TM, FFN_TF
    tiles_per_seq = seq // tm
    halo_blocks = tm // HALO
    last_halo = tokens // HALO - 1
    wrows = tm + 2 * HALO
    n_tiles = tokens // tm
    buffers = (3 * tm * d_model * 4 + tm * d_model * 2 + wrows * d_model * 2 + 2 * wrows * tf * 4
               + 2 * 3 * d_model * tf * 2)
    assert buffers < VMEM_LIMIT_BYTES, (buffers, VMEM_LIMIT_BYTES)
    a_scratch = pltpu.VMEM((tf // LANES, wrows, LANES), _F32)
    any_spec = pl.BlockSpec(memory_space=pl.ANY)
    return pl.pallas_call(
        functools.partial(_ffn_body, layer=layer, tiles_per_seq=tiles_per_seq, n_tiles=n_tiles, nj=d_ff // tf),
        grid=(n_tiles,),
        in_specs=[
            pl.BlockSpec((HALO, d_model), lambda m: (jnp.maximum(m * halo_blocks - 1, 0), 0)),
            any_spec,
            pl.BlockSpec((HALO, d_model), lambda m: (jnp.minimum((m + 1) * halo_blocks, last_halo), 0)),
            any_spec,
            any_spec,
            _layer_block(conv_w, layer),
            _resident(conv_b),
            any_spec,
        ],
        out_specs=pl.BlockSpec((tm, d_model), lambda m: (m, 0)),
        out_shape=jax.ShapeDtypeStruct((tokens, d_model), _F32),
        scratch_shapes=[pltpu.VMEM((wrows, d_model), _BF16), a_scratch, a_scratch,
                        pltpu.VMEM((tm, d_model), _BF16), pltpu.VMEM((tm, d_model), _F32),
                        pltpu.VMEM((2, d_model, tf), _BF16), pltpu.VMEM((2, d_model, tf), _BF16),
                        pltpu.VMEM((2, tf, d_model), _BF16),
                        pltpu.SemaphoreType.DMA((2,)), pltpu.SemaphoreType.DMA((3, 2))],
        compiler_params=pltpu.CompilerParams(dimension_semantics=("arbitrary",),
                                             vmem_limit_bytes=VMEM_LIMIT_BYTES),
        name="ffn",
    )(hn, hn, hn, x2, w_up, conv_w, conv_b, w_down)


def _rope_tables(seq):
    inv_freq = ROPE_THETA ** (-np.arange(0, HEAD_DIM, 2, dtype=np.float64) / HEAD_DIM)
    ang = np.arange(seq, dtype=np.float64)[:, None] * inv_freq[None, :]
    cos, sin = np.cos(ang), np.sin(ang)
    table = lambda a, b: jnp.asarray(np.concatenate([a, b], axis=-1).astype(np.float32))
    return table(cos, cos), table(-sin, sin)


def kernel(x, norm1_g, w_in, q_norm_g, k_norm_g, sink, sgu_ln_g, sgu_ln_b, w_s, b_s,
           attn_out_g, sgu_out_g, w_o, norm2_g, w_up, conv_w, conv_b, w_down):
    batch, seq, d_model = x.shape
    depth = w_in.shape[0]
    attn_w = attn_out_g.shape[1]
    gmlp_w = sgu_out_g.shape[1]
    kv_w = (w_in.shape[2] - attn_w - 2 * gmlp_w) // 2
    assert seq % max(IN_TM, MIX_TM, FFN_TM) == 0 and w_down.shape[1] % FFN_TF == 0
    assert w_s.shape[2] == BLOCK and seq >= BAND and MIX_TM % (MIX_GROUP_BLOCKS * BLOCK) == 0

    cosf, sinf = _rope_tables(seq)
    g1, qg, kg, lng, lnb = norm1_g, q_norm_g, k_norm_g, sgu_ln_g, sgu_ln_b
    ag, sg, g2, cb = attn_out_g, sgu_out_g, norm2_g, conv_b
    bst = jnp.swapaxes(b_s, 1, 2)
    ws = w_s.astype(_BF16)
    win = w_in[0].astype(_BF16)

    x2 = x.reshape(batch * seq, d_model)
    for l in range(depth):
        to_cast = [(w_o, l), (w_up, l), (w_down, l)] + ([(w_in, l + 1)] if l + 1 < depth else [])
        q, k, v, u, vn, wo, wup, wdn, *nxt = _in_proj(l, x2, g1, win, qg, kg, cosf, sinf, lng, lnb, to_cast,
                                                      seq=seq, attn_w=attn_w, kv_w=kv_w, gmlp_w=gmlp_w)
        win = nxt[0] if nxt else None
        x2, hn = _mixer(l, sink, q, k, v, u, vn, x2, wo, ws, bst, ag, sg, g2, batch=batch, seq=seq)
        x2 = _ffn(l, hn, x2, wup, conv_w, cb, wdn, seq=seq)
    return x2.reshape(batch, seq, d_model)
```

```python
import functools

import jax
import jax.numpy as jnp
import numpy as np
from jax import lax
from jax.experimental import pallas as pl
from jax.experimental.pallas import tpu as pltpu

HEAD_DIM = 128
GQA_GROUP = 4
WINDOW = 128
BLOCK = 128
BAND = 3 * BLOCK
ROPE_THETA = 10000.0
EPS = 1e-6
MASK_VALUE = -1e30
LOG2E = 1.4426950408889634

BF16_TILE_ROWS = 16
LANES = 128
HALO = BF16_TILE_ROWS
VMEM_LIMIT_BYTES = 56 * 1024 * 1024

IN_TM = 512
MIX_TM = 512
MIX_GROUP_BLOCKS = 2
FFN_TM = 1024
FFN_TF = 512

_BF16 = jnp.bfloat16
_F32 = jnp.float32


def _dot(a, b):
    return jnp.dot(a, b, preferred_element_type=_F32)


def _rms(x, g):
    return x * lax.rsqrt(jnp.mean(x * x, axis=-1, keepdims=True) + EPS) * g


def _gelu_tanh(x):
    c = np.float32(np.sqrt(2.0 / np.pi))
    return x * (0.5 * (1.0 + jnp.tanh(c * (x + np.float32(0.044715) * (x * x * x)))))


def _resident(arr):
    zeros = (0,) * arr.ndim
    return pl.BlockSpec(arr.shape, lambda *_: zeros, pipeline_mode=pl.Buffered(1))


def _layer_block(arr, layer):
    tail = (0,) * (arr.ndim - 1)
    return pl.BlockSpec((None,) + arr.shape[1:], lambda *_: (layer,) + tail, pipeline_mode=pl.Buffered(1))


def _in_proj_body(x_ref, g1_ref, win_ref, qg_ref, kg_ref, cos_ref, sin_ref, lng_ref, lnb_ref, *refs,
                  layer, n_cast, attn_w, kv_w, gmlp_w):
    row = lambda ref: ref[layer:layer + 1, :]
    cast_in = refs[:n_cast]
    q_ref, k_ref, v_ref, u_ref, vn_ref = refs[n_cast:n_cast + 5]
    cast_out = refs[n_cast + 5:]
    for src, dst in zip(cast_in, cast_out):
        if len(dst.shape) == 2:
            dst[...] = src[...].astype(_BF16)
        else:
            for t in range(dst.shape[0]):
                dst[t] = src[:, t * dst.shape[2]:(t + 1) * dst.shape[2]].astype(_BF16)
    h = _rms(x_ref[...], row(g1_ref)).astype(_BF16)
    cosf = cos_ref[...]
    sinf = sin_ref[...]

    def qk_heads(col0, n_heads, gain, out_ref):
        y = _dot(h, win_ref[:, col0:col0 + n_heads * HEAD_DIM])
        for hd in range(n_heads):
            sl = slice(hd * HEAD_DIM, (hd + 1) * HEAD_DIM)
            t = _rms(y[:, sl], gain)
            out_ref[:, sl] = (t * cosf + pltpu.roll(t, HEAD_DIM // 2, 1) * sinf).astype(_BF16)

    c_v = attn_w + kv_w
    c_u = c_v + kv_w
    c_gv = c_u + gmlp_w
    qk_heads(0, attn_w // HEAD_DIM, row(qg_ref), q_ref)
    qk_heads(attn_w, kv_w // HEAD_DIM, row(kg_ref), k_ref)
    v_ref[...] = _dot(h, win_ref[:, c_v:c_v + kv_w]).astype(_BF16)
    gv = _gelu_tanh(_dot(h, win_ref[:, c_gv:c_gv + gmlp_w]))
    xc = gv - jnp.mean(gv, axis=-1, keepdims=True)
    vn = xc * lax.rsqrt(jnp.mean(xc * xc, axis=-1, keepdims=True) + EPS) * row(lng_ref) + row(lnb_ref)
    vn_ref[...] = vn.astype(_BF16)
    half = gmlp_w // 2
    for c0 in (0, half):
        u_ref[:, c0:c0 + half] = _gelu_tanh(_dot(h, win_ref[:, c_u + c0:c_u + c0 + half])).astype(_BF16)


def _in_proj(layer, x2, g1, win, qg, kg, cosf, sinf, lng, lnb, to_cast, *, seq, attn_w, kv_w, gmlp_w):
    tokens, d_model = x2.shape
    tm = IN_TM
    steps = tokens // tm
    tiles_per_seq = seq // tm
    row = lambda w: pl.BlockSpec((tm, w), lambda m: (m, 0))
    pos = pl.BlockSpec((tm, HEAD_DIM), lambda m: (m % tiles_per_seq, 0))
    cast_in, cast_out, cast_shapes = [], [], []
    for w, wl, col_tile in to_cast:
        assert w.shape[1] % (steps * BF16_TILE_ROWS) == 0
        slab = (w.shape[1] // steps, w.shape[2])
        cast_in.append(pl.BlockSpec((None,) + slab, lambda m, wl=wl: (wl, m, 0)))
        if col_tile is None:
            cast_out.append(pl.BlockSpec(slab, lambda m: (m, 0)))
            cast_shapes.append(jax.ShapeDtypeStruct(w.shape[1:], _BF16))
        else:
            n_col = w.shape[2] // col_tile
            cast_out.append(pl.BlockSpec((n_col, slab[0], col_tile), lambda m: (0, m, 0)))
            cast_shapes.append(jax.ShapeDtypeStruct((n_col, w.shape[1], col_tile), _BF16))
    return pl.pallas_call(
        functools.partial(_in_proj_body, layer=layer, n_cast=len(to_cast), attn_w=attn_w, kv_w=kv_w, gmlp_w=gmlp_w),
        grid=(steps,),
        in_specs=[row(d_model), _resident(g1), _resident(win), _resident(qg), _resident(kg), pos, pos,
                  _resident(lng), _resident(lnb)] + cast_in,
        out_specs=[row(attn_w), row(kv_w), row(kv_w), row(gmlp_w), row(gmlp_w)] + cast_out,
        out_shape=[jax.ShapeDtypeStruct((tokens, w), _BF16) for w in (attn_w, kv_w, kv_w, gmlp_w, gmlp_w)]
        + cast_shapes,
        compiler_params=pltpu.CompilerParams(dimension_semantics=("parallel",),
                                             vmem_limit_bytes=VMEM_LIMIT_BYTES),
        name="in_proj",
    )(x2, g1, win, qg, kg, cosf, sinf, lng, lnb, *(w for w, _, _ in to_cast))


def _mixer_body(sink_ref, q_ref, k_ref, v_ref, u_ref, vn_ref, x_ref, wo_ref, ws_ref, bst_ref,
                ag_ref, sg_ref, g2_ref, xo_ref, hn_ref, attn_scr, sgu_scr, mix_scr,
                *, layer, seq, n_kv, n_gmlp):
    tm = q_ref.shape[0]
    attn_w = q_ref.shape[1]
    blocks = tm // BLOCK
    first_block = pl.program_id(1) * blocks
    scale_log2e = HEAD_DIM ** -0.5 * LOG2E
    row = lambda ref: ref[layer:layer + 1, :]

    for i in range(blocks):
        rs = slice(i * BLOCK, (i + 1) * BLOCK)
        n = first_block + i
        ks = pl.multiple_of(jnp.clip((n - 1) * BLOCK, 0, seq - BAND), BLOCK)
        rel = (ks - n * BLOCK) + (lax.broadcasted_iota(jnp.int32, (BLOCK, BAND), 1)
                                  - lax.broadcasted_iota(jnp.int32, (BLOCK, BAND), 0))
        valid = jnp.abs(rel) <= WINDOW
        for g in range(n_kv):
            kv_sl = slice(g * HEAD_DIM, (g + 1) * HEAD_DIM)
            kb = k_ref[pl.ds(ks, BAND), kv_sl]
            vb = v_ref[pl.ds(ks, BAND), kv_sl]
            heads = range(g * GQA_GROUP, (g + 1) * GQA_GROUP)
            q4 = jnp.concatenate([q_ref[rs, hd * HEAD_DIM:(hd + 1) * HEAD_DIM] for hd in heads], axis=0)
            s4 = lax.dot_general(q4, kb, (((1,), (1,)), ((), ())), preferred_element_type=_F32)
            ps, denoms = [], []
            for j, hd in enumerate(heads):
                s = jnp.where(valid, s4[j * BLOCK:(j + 1) * BLOCK] * scale_log2e, MASK_VALUE)
                sk = sink_ref[layer, hd] * LOG2E
                m = jnp.maximum(jnp.max(s, axis=-1, keepdims=True), sk)
                p = jnp.exp2(s - m)
                denoms.append(jnp.sum(p, axis=-1, keepdims=True) + jnp.exp2(sk - m))
                ps.append(p.astype(_BF16))
            o4 = _dot(jnp.concatenate(ps, axis=0), vb)
            for j, hd in enumerate(heads):
                attn_scr[rs, hd * HEAD_DIM:(hd + 1) * HEAD_DIM] = o4[j * BLOCK:(j + 1) * BLOCK] / denoms[j]

    for i in range(blocks):
        rs = slice(i * BLOCK, (i + 1) * BLOCK)
        for hd in range(n_gmlp):
            sl = slice(hd * HEAD_DIM, (hd + 1) * HEAD_DIM)
            f = _dot(ws_ref[hd], vn_ref[rs, sl]) + bst_ref[:, hd:hd + 1]
            sgu_scr[rs, sl] = u_ref[rs, sl].astype(_F32) * f

    group = MIX_GROUP_BLOCKS * BLOCK
    for r0 in range(0, tm, group):
        rows = slice(r0, r0 + group)
        mix_scr[rows, :attn_w] = _rms(attn_scr[rows, :], row(ag_ref)).astype(_BF16)
        mix_scr[rows, attn_w:] = _rms(sgu_scr[rows, :], row(sg_ref)).astype(_BF16)
        x_new = x_ref[rows, :] + _dot(mix_scr[rows, :], wo_ref[...])
        xo_ref[rows, :] = x_new
        hn_ref[rows, :] = _rms(x_new, row(g2_ref)).astype(_BF16)


def _mixer(layer, sink, q, k, v, u, vn, x2, wo, ws, bst, ag, sg, g2, *, batch, seq):
    tokens, d_model = x2.shape
    attn_w, kv_w, gmlp_w = q.shape[1], k.shape[1], u.shape[1]
    tm = MIX_TM
    tiles = seq // tm
    row = lambda w: pl.BlockSpec((tm, w), lambda b, t: (b * tiles + t, 0))
    whole_seq = pl.BlockSpec((seq, kv_w), lambda b, t: (b, 0))
    lb = functools.partial(_layer_block, layer=layer)
    body = functools.partial(_mixer_body, layer=layer, seq=seq, n_kv=kv_w // HEAD_DIM,
                             n_gmlp=gmlp_w // HEAD_DIM)
    return pl.pallas_call(
        body,
        grid=(batch, tiles),
        in_specs=[pl.BlockSpec(memory_space=pltpu.SMEM), row(attn_w), whole_seq, whole_seq, row(gmlp_w),
                  row(gmlp_w), row(d_model), _resident(wo), lb(ws), lb(bst), _resident(ag), _resident(sg),
                  _resident(g2)],
        out_specs=[row(d_model), row(d_model)],
        out_shape=[jax.ShapeDtypeStruct((tokens, d_model), _F32),
                   jax.ShapeDtypeStruct((tokens, d_model), _BF16)],
        scratch_shapes=[pltpu.VMEM((tm, attn_w), _F32), pltpu.VMEM((tm, gmlp_w), _F32),
                        pltpu.VMEM((tm, d_model), _BF16)],
        compiler_params=pltpu.CompilerParams(dimension_semantics=("parallel", "parallel"),
                                             vmem_limit_bytes=VMEM_LIMIT_BYTES),
        name="mixer",
    )(sink, q, k, v, u, vn, x2, wo, ws, bst, ag, sg, g2)


def _ffn_body(prev_ref, hn_hbm, next_ref, x_hbm, wup_hbm, cw_ref, cb_ref, wdn_hbm,
              o_ref, win_scr, ag_scr, au_scr, main_scr, x_scr, wg_buf, wu_buf, wd_buf, tile_sems, w_sems,
              *, layer, tiles_per_seq, n_tiles, nj):
    tm = main_scr.shape[0]
    tf = wg_buf.shape[2]
    d_ff = nj * tf
    slabs = ag_scr.shape[0]
    m = pl.program_id(0)

    def tile_copies(tile):
        rows = pl.ds(pl.multiple_of(tile * tm, tm), tm)
        return (pltpu.make_async_copy(hn_hbm.at[rows, :], main_scr, tile_sems.at[0]),
                pltpu.make_async_copy(x_hbm.at[rows, :], x_scr, tile_sems.at[1]))

    def weight_copies(jj, slot):
        c0 = pl.multiple_of(jj * tf, tf)
        return (pltpu.make_async_copy(wup_hbm.at[jj], wg_buf.at[slot], w_sems.at[0, slot]),
                pltpu.make_async_copy(wup_hbm.at[nj + jj], wu_buf.at[slot], w_sems.at[1, slot]),
                pltpu.make_async_copy(wdn_hbm.at[pl.ds(c0, tf), :], wd_buf.at[slot], w_sems.at[2, slot]))

    @pl.when(m == 0)
    def _():
        for copy in tile_copies(m) + weight_copies(0, 0):
            copy.start()

    for copy in tile_copies(m):
        copy.wait()
    t = m % tiles_per_seq
    win_scr[:HALO] = jnp.where(t == 0, jnp.zeros_like(prev_ref[...]), prev_ref[...])
    win_scr[HALO:HALO + tm] = main_scr[...]
    win_scr[HALO + tm:] = jnp.where(t == tiles_per_seq - 1, jnp.zeros_like(next_ref[...]), next_ref[...])
    o_ref[...] = x_scr[...]

    @pl.when(m + 1 < n_tiles)
    def _():
        for copy in tile_copies(m + 1):
            copy.start()

    def step(j, carry):
        g = m * nj + j
        slot = lax.rem(g, 2)
        for copy in weight_copies(j, slot):
            copy.wait()

        @pl.when(g + 1 < n_tiles * nj)
        def _():
            for copy in weight_copies(jnp.where(j + 1 == nj, 0, j + 1), 1 - slot):
                copy.start()

        win = win_scr[...]
        for w_buf, a_scr in ((wg_buf, ag_scr), (wu_buf, au_scr)):
            a = _dot(win, w_buf[slot])
            for s in range(slabs):
                a_scr[s] = a[:, s * LANES:(s + 1) * LANES]

        def conv(a_scr, col0, s):
            cols = pl.ds(pl.multiple_of(col0 + j * tf + s * LANES, LANES), LANES)
            below = a_scr[s, pl.ds(HALO - 1, tm, stride=1), :]
            mid = a_scr[s, pl.ds(HALO, tm), :]
            above = a_scr[s, pl.ds(HALO + 1, tm, stride=1), :]
            return (cb_ref[layer:layer + 1, cols] + below * cw_ref[0:1, cols] + mid * cw_ref[1:2, cols]
                    + above * cw_ref[2:3, cols])

        acts = []
        for s in range(slabs):
            gate = conv(ag_scr, 0, s)
            up = conv(au_scr, d_ff, s)
            acts.append((gate * (1.0 / (1.0 + jnp.exp(-gate))) * up).astype(_BF16))
        o_ref[...] += _dot(jnp.concatenate(acts, axis=1), wd_buf[slot])
        return carry

    lax.fori_loop(0, nj, step, 0)


def _ffn(layer, hn, x2, w_up, conv_w, conv_b, w_down, *, seq):
    tokens, d_model = x2.shape
    d_ff = w_down.shape[0]
    tm, tf = FFN_TM, FFN_TF
    tiles_per_seq = seq // tm
    halo_blocks = tm // HALO
    last_halo = tokens // HALO - 1
    wrows = tm + 2 * HALO
    n_tiles = tokens // tm
    buffers = (3 * tm * d_model * 4 + tm * d_model * 2 + wrows * d_model * 2 + 2 * wrows * tf * 4
               + 2 * 3 * d_model * tf * 2)
    assert buffers < VMEM_LIMIT_BYTES, (buffers, VMEM_LIMIT_BYTES)
    a_scratch = pltpu.VMEM((tf // LANES, wrows, LANES), _F32)
    any_spec = pl.BlockSpec(memory_space=pl.ANY)
    return pl.pallas_call(
        functools.partial(_ffn_body, layer=layer, tiles_per_seq=tiles_per_seq, n_tiles=n_tiles, nj=d_ff // tf),
        grid=(n_tiles,),
        in_specs=[
            pl.BlockSpec((HALO, d_model), lambda m: (jnp.maximum(m * halo_blocks - 1, 0), 0)),
            any_spec,
            pl.BlockSpec((HALO, d_model), lambda m: (jnp.minimum((m + 1) * halo_blocks, last_halo), 0)),
            any_spec,
            any_spec,
            _layer_block(conv_w, layer),
            _resident(conv_b),
            any_spec,
        ],
        out_specs=pl.BlockSpec((tm, d_model), lambda m: (m, 0)),
        out_shape=jax.ShapeDtypeStruct((tokens, d_model), _F32),
        scratch_shapes=[pltpu.VMEM((wrows, d_model), _BF16), a_scratch, a_scratch,
                        pltpu.VMEM((tm, d_model), _BF16), pltpu.VMEM((tm, d_model), _F32),
                        pltpu.VMEM((2, d_model, tf), _BF16), pltpu.VMEM((2, d_model, tf), _BF16),
                        pltpu.VMEM((2, tf, d_model), _BF16),
                        pltpu.SemaphoreType.DMA((2,)), pltpu.SemaphoreType.DMA((3, 2))],
        compiler_params=pltpu.CompilerParams(dimension_semantics=("arbitrary",),
                                             vmem_limit_bytes=VMEM_LIMIT_BYTES),
        name="ffn",
    )(hn, hn, hn, x2, w_up, conv_w, conv_b, w_down)


def _rope_tables(seq):
    inv_freq = ROPE_THETA ** (-np.arange(0, HEAD_DIM, 2, dtype=np.float64) / HEAD_DIM)
    ang = np.arange(seq, dtype=np.float64)[:, None] * inv_freq[None, :]
    cos, sin = np.cos(ang), np.sin(ang)
    table = lambda a, b: jnp.asarray(np.concatenate([a, b], axis=-1).astype(np.float32))
    return table(cos, cos), table(-sin, sin)


def kernel(x, norm1_g, w_in, q_norm_g, k_norm_g, sink, sgu_ln_g, sgu_ln_b, w_s, b_s,
           attn_out_g, sgu_out_g, w_o, norm2_g, w_up, conv_w, conv_b, w_down):
    batch, seq, d_model = x.shape
    depth = w_in.shape[0]
    attn_w = attn_out_g.shape[1]
    gmlp_w = sgu_out_g.shape[1]
    kv_w = (w_in.shape[2] - attn_w - 2 * gmlp_w) // 2
    assert seq % max(IN_TM, MIX_TM, FFN_TM) == 0 and w_down.shape[1] % FFN_TF == 0
    assert w_s.shape[2] == BLOCK and seq >= BAND and MIX_TM % (MIX_GROUP_BLOCKS * BLOCK) == 0

    cosf, sinf = _rope_tables(seq)
    g1, qg, kg, lng, lnb = norm1_g, q_norm_g, k_norm_g, sgu_ln_g, sgu_ln_b
    ag, sg, g2, cb = attn_out_g, sgu_out_g, norm2_g, conv_b
    bst = jnp.swapaxes(b_s, 1, 2)
    ws = w_s.astype(_BF16)
    win = w_in[0].astype(_BF16)

    x2 = x.reshape(batch * seq, d_model)
    for l in range(depth):
        to_cast = ([(w_o, l, None), (w_up, l, FFN_TF), (w_down, l, None)]
                   + ([(w_in, l + 1, None)] if l + 1 < depth else []))
        q, k, v, u, vn, wo, wup, wdn, *nxt = _in_proj(l, x2, g1, win, qg, kg, cosf, sinf, lng, lnb, to_cast,
                                                      seq=seq, attn_w=attn_w, kv_w=kv_w, gmlp_w=gmlp_w)
        win = nxt[0] if nxt else None
        x2, hn = _mixer(l, sink, q, k, v, u, vn, x2, wo, ws, bst, ag, sg, g2, batch=batch, seq=seq)
        x2 = _ffn(l, hn, x2, wup, conv_w, cb, wdn, seq=seq)
    return x2.reshape(batch, seq, d_model)
```

```python
import functools

import jax
import jax.numpy as jnp
import numpy as np
from jax import lax
from jax.experimental import pallas as pl
from jax.experimental.pallas import tpu as pltpu

HEAD_DIM = 128
GQA_GROUP = 4
WINDOW = 128
BLOCK = 128
BAND = 3 * BLOCK
ROPE_THETA = 10000.0
EPS = 1e-6
MASK_VALUE = -1e30
LOG2E = 1.4426950408889634

BF16_TILE_ROWS = 16
LANES = 128
HALO = BF16_TILE_ROWS
VMEM_LIMIT_BYTES = 56 * 1024 * 1024

IN_TM = 512
MIX_TM = 512
MIX_GROUP_BLOCKS = 2
FFN_TM = 1024
FFN_TF = 512

_BF16 = jnp.bfloat16
_F32 = jnp.float32


def _dot(a, b):
    return jnp.dot(a, b, preferred_element_type=_F32)


def _rms(x, g):
    return x * lax.rsqrt(jnp.mean(x * x, axis=-1, keepdims=True) + EPS) * g


def _gelu_tanh(x):
    c = np.float32(np.sqrt(2.0 / np.pi))
    return x * (0.5 * (1.0 + jnp.tanh(c * (x + np.float32(0.044715) * (x * x * x)))))


def _resident(arr):
    zeros = (0,) * arr.ndim
    return pl.BlockSpec(arr.shape, lambda *_: zeros, pipeline_mode=pl.Buffered(1))


def _layer_block(arr, layer):
    tail = (0,) * (arr.ndim - 1)
    return pl.BlockSpec((None,) + arr.shape[1:], lambda *_: (layer,) + tail, pipeline_mode=pl.Buffered(1))


def _in_proj_body(x_ref, g1_ref, win_ref, qg_ref, kg_ref, cos_ref, sin_ref, lng_ref, lnb_ref, *refs,
                  layer, n_cast, attn_w, kv_w, gmlp_w):
    row = lambda ref: ref[layer:layer + 1, :]
    cast_in = refs[:n_cast]
    q_ref, k_ref, v_ref, u_ref, vn_ref = refs[n_cast:n_cast + 5]
    cast_out = refs[n_cast + 5:]
    for src, dst in zip(cast_in, cast_out):
        dst[...] = src[...].astype(_BF16)
    h = _rms(x_ref[...], row(g1_ref)).astype(_BF16)
    cosf = cos_ref[...]
    sinf = sin_ref[...]

    def qk_heads(col0, n_heads, gain, out_ref):
        y = _dot(h, win_ref[:, col0:col0 + n_heads * HEAD_DIM])
        for hd in range(n_heads):
            sl = slice(hd * HEAD_DIM, (hd + 1) * HEAD_DIM)
            t = _rms(y[:, sl], gain)
            out_ref[:, sl] = (t * cosf + pltpu.roll(t, HEAD_DIM // 2, 1) * sinf).astype(_BF16)

    c_v = attn_w + kv_w
    c_u = c_v + kv_w
    c_gv = c_u + gmlp_w
    qk_heads(0, attn_w // HEAD_DIM, row(qg_ref), q_ref)
    qk_heads(attn_w, kv_w // HEAD_DIM, row(kg_ref), k_ref)
    v_ref[...] = _dot(h, win_ref[:, c_v:c_v + kv_w]).astype(_BF16)
    half_rows = h.shape[0] // 2
    for r0 in (0, half_rows):
        gv = _gelu_tanh(_dot(h[r0:r0 + half_rows], win_ref[:, c_gv:c_gv + gmlp_w]))
        xc = gv - jnp.mean(gv, axis=-1, keepdims=True)
        vn = xc * lax.rsqrt(jnp.mean(xc * xc, axis=-1, keepdims=True) + EPS) * row(lng_ref) + row(lnb_ref)
        vn_ref[r0:r0 + half_rows, :] = vn.astype(_BF16)
    half = gmlp_w // 2
    for c0 in (0, half):
        u_ref[:, c0:c0 + half] = _gelu_tanh(_dot(h, win_ref[:, c_u + c0:c_u + c0 + half])).astype(_BF16)


def _in_proj(layer, x2, g1, win, qg, kg, cosf, sinf, lng, lnb, to_cast, *, seq, attn_w, kv_w, gmlp_w):
    tokens, d_model = x2.shape
    tm = IN_TM
    steps = tokens // tm
    tiles_per_seq = seq // tm
    row = lambda w: pl.BlockSpec((tm, w), lambda m: (m, 0))
    pos = pl.BlockSpec((tm, HEAD_DIM), lambda m: (m % tiles_per_seq, 0))
    cast_in, cast_out, cast_shapes = [], [], []
    for w, wl in to_cast:
        assert w.shape[1] % (steps * BF16_TILE_ROWS) == 0
        slab = (w.shape[1] // steps, w.shape[2])
        cast_in.append(pl.BlockSpec((None,) + slab, lambda m, wl=wl: (wl, m, 0)))
        cast_out.append(pl.BlockSpec(slab, lambda m: (m, 0)))
        cast_shapes.append(jax.ShapeDtypeStruct(w.shape[1:], _BF16))
    return pl.pallas_call(
        functools.partial(_in_proj_body, layer=layer, n_cast=len(to_cast), attn_w=attn_w, kv_w=kv_w, gmlp_w=gmlp_w),
        grid=(steps,),
        in_specs=[row(d_model), _resident(g1), _resident(win), _resident(qg), _resident(kg), pos, pos,
                  _resident(lng), _resident(lnb)] + cast_in,
        out_specs=[row(attn_w), row(kv_w), row(kv_w), row(gmlp_w), row(gmlp_w)] + cast_out,
        out_shape=[jax.ShapeDtypeStruct((tokens, w), _BF16) for w in (attn_w, kv_w, kv_w, gmlp_w, gmlp_w)]
        + cast_shapes,
        compiler_params=pltpu.CompilerParams(dimension_semantics=("parallel",),
                                             vmem_limit_bytes=VMEM_LIMIT_BYTES),
        name="in_proj",
    )(x2, g1, win, qg, kg, cosf, sinf, lng, lnb, *(w for w, _ in to_cast))


def _mixer_body(sink_ref, q_ref, k_ref, v_ref, u_ref, vn_ref, x_ref, wo_ref, ws_ref, bst_ref,
                ag_ref, sg_ref, g2_ref, xo_ref, hn_ref, attn_scr, sgu_scr, mix_scr,
                *, layer, seq, n_kv, n_gmlp):
    tm = q_ref.shape[0]
    attn_w = q_ref.shape[1]
    blocks = tm // BLOCK
    first_block = pl.program_id(1) * blocks
    scale_log2e = HEAD_DIM ** -0.5 * LOG2E
    row = lambda ref: ref[layer:layer + 1, :]

    for i in range(blocks):
        rs = slice(i * BLOCK, (i + 1) * BLOCK)
        n = first_block + i
        ks = pl.multiple_of(jnp.clip((n - 1) * BLOCK, 0, seq - BAND), BLOCK)
        rel = (ks - n * BLOCK) + (lax.broadcasted_iota(jnp.int32, (BLOCK, BAND), 1)
                                  - lax.broadcasted_iota(jnp.int32, (BLOCK, BAND), 0))
        valid = jnp.abs(rel) <= WINDOW
        for g in range(n_kv):
            kv_sl = slice(g * HEAD_DIM, (g + 1) * HEAD_DIM)
            kb = k_ref[pl.ds(ks, BAND), kv_sl]
            vb = v_ref[pl.ds(ks, BAND), kv_sl]
            heads = range(g * GQA_GROUP, (g + 1) * GQA_GROUP)
            q4 = jnp.concatenate([q_ref[rs, hd * HEAD_DIM:(hd + 1) * HEAD_DIM] for hd in heads], axis=0)
            s4 = lax.dot_general(q4, kb, (((1,), (1,)), ((), ())), preferred_element_type=_F32)
            ps, denoms = [], []
            for j, hd in enumerate(heads):
                s = jnp.where(valid, s4[j * BLOCK:(j + 1) * BLOCK] * scale_log2e, MASK_VALUE)
                sk = sink_ref[layer, hd] * LOG2E
                m = jnp.maximum(jnp.max(s, axis=-1, keepdims=True), sk)
                p = jnp.exp2(s - m)
                denoms.append(jnp.sum(p, axis=-1, keepdims=True) + jnp.exp2(sk - m))
                ps.append(p.astype(_BF16))
            o4 = _dot(jnp.concatenate(ps, axis=0), vb)
            for j, hd in enumerate(heads):
                attn_scr[rs, hd * HEAD_DIM:(hd + 1) * HEAD_DIM] = o4[j * BLOCK:(j + 1) * BLOCK] / denoms[j]

    for i in range(blocks):
        rs = slice(i * BLOCK, (i + 1) * BLOCK)
        for hd in range(n_gmlp):
            sl = slice(hd * HEAD_DIM, (hd + 1) * HEAD_DIM)
            f = _dot(ws_ref[hd], vn_ref[rs, sl]) + bst_ref[:, hd:hd + 1]
            sgu_scr[rs, sl] = u_ref[rs, sl].astype(_F32) * f

    group = MIX_GROUP_BLOCKS * BLOCK
    for r0 in range(0, tm, group):
        rows = slice(r0, r0 + group)
        mix_scr[rows, :attn_w] = _rms(attn_scr[rows, :], row(ag_ref)).astype(_BF16)
        mix_scr[rows, attn_w:] = _rms(sgu_scr[rows, :], row(sg_ref)).astype(_BF16)
        x_new = x_ref[rows, :] + _dot(mix_scr[rows, :], wo_ref[...])
        xo_ref[rows, :] = x_new
        hn_ref[rows, :] = _rms(x_new, row(g2_ref)).astype(_BF16)


def _mixer(layer, sink, q, k, v, u, vn, x2, wo, ws, bst, ag, sg, g2, *, batch, seq):
    tokens, d_model = x2.shape
    attn_w, kv_w, gmlp_w = q.shape[1], k.shape[1], u.shape[1]
    tm = MIX_TM
    tiles = seq // tm
    row = lambda w: pl.BlockSpec((tm, w), lambda b, t: (b * tiles + t, 0))
    whole_seq = pl.BlockSpec((seq, kv_w), lambda b, t: (b, 0))
    lb = functools.partial(_layer_block, layer=layer)
    body = functools.partial(_mixer_body, layer=layer, seq=seq, n_kv=kv_w // HEAD_DIM,
                             n_gmlp=gmlp_w // HEAD_DIM)
    return pl.pallas_call(
        body,
        grid=(batch, tiles),
        in_specs=[pl.BlockSpec(memory_space=pltpu.SMEM), row(attn_w), whole_seq, whole_seq, row(gmlp_w),
                  row(gmlp_w), row(d_model), _resident(wo), lb(ws), lb(bst), _resident(ag), _resident(sg),
                  _resident(g2)],
        out_specs=[row(d_model), row(d_model)],
        out_shape=[jax.ShapeDtypeStruct((tokens, d_model), _F32),
                   jax.ShapeDtypeStruct((tokens, d_model), _BF16)],
        scratch_shapes=[pltpu.VMEM((tm, attn_w), _F32), pltpu.VMEM((tm, gmlp_w), _F32),
                        pltpu.VMEM((tm, d_model), _BF16)],
        compiler_params=pltpu.CompilerParams(dimension_semantics=("parallel", "parallel"),
                                             vmem_limit_bytes=VMEM_LIMIT_BYTES),
        name="mixer",
    )(sink, q, k, v, u, vn, x2, wo, ws, bst, ag, sg, g2)


def _ffn_body(prev_ref, hn_hbm, next_ref, x_hbm, wup_hbm, cw_ref, cb_ref, wdn_hbm,
              o_ref, win_scr, ag_scr, au_scr, main_scr, x_scr, wg_buf, wu_buf, wd_buf, tile_sems, w_sems,
              *, layer, tiles_per_seq, n_tiles, nj):
    tm = main_scr.shape[0]
    tf = wg_buf.shape[2]
    d_ff = nj * tf
    slabs = ag_scr.shape[0]
    m = pl.program_id(0)

    def tile_copies(tile):
        rows = pl.ds(pl.multiple_of(tile * tm, tm), tm)
        return (pltpu.make_async_copy(hn_hbm.at[rows, :], main_scr, tile_sems.at[0]),
                pltpu.make_async_copy(x_hbm.at[rows, :], x_scr, tile_sems.at[1]))

    def weight_copies(jj, slot):
        c0 = pl.multiple_of(jj * tf, tf)
        return (pltpu.make_async_copy(wup_hbm.at[:, pl.ds(c0, tf)], wg_buf.at[slot], w_sems.at[0, slot]),
                pltpu.make_async_copy(wup_hbm.at[:, pl.ds(d_ff + c0, tf)], wu_buf.at[slot], w_sems.at[1, slot]),
                pltpu.make_async_copy(wdn_hbm.at[pl.ds(c0, tf), :], wd_buf.at[slot], w_sems.at[2, slot]))

    @pl.when(m == 0)
    def _():
        for copy in tile_copies(m) + weight_copies(0, 0):
            copy.start()

    for copy in tile_copies(m):
        copy.wait()
    t = m % tiles_per_seq
    win_scr[:HALO] = jnp.where(t == 0, jnp.zeros_like(prev_ref[...]), prev_ref[...])
    win_scr[HALO:HALO + tm] = main_scr[...]
    win_scr[HALO + tm:] = jnp.where(t == tiles_per_seq - 1, jnp.zeros_like(next_ref[...]), next_ref[...])
    o_ref[...] = x_scr[...]

    @pl.when(m + 1 < n_tiles)
    def _():
        for copy in tile_copies(m + 1):
            copy.start()

    def step(j, carry):
        g = m * nj + j
        slot = lax.rem(g, 2)
        for copy in weight_copies(j, slot):
            copy.wait()

        @pl.when(g + 1 < n_tiles * nj)
        def _():
            for copy in weight_copies(jnp.where(j + 1 == nj, 0, j + 1), 1 - slot):
                copy.start()

        win = win_scr[...]
        for w_buf, a_scr in ((wg_buf, ag_scr), (wu_buf, au_scr)):
            a = _dot(win, w_buf[slot])
            for s in range(slabs):
                a_scr[s] = a[:, s * LANES:(s + 1) * LANES]

        def conv(a_scr, col0, s):
            cols = pl.ds(pl.multiple_of(col0 + j * tf + s * LANES, LANES), LANES)
            below = a_scr[s, pl.ds(HALO - 1, tm, stride=1), :]
            mid = a_scr[s, pl.ds(HALO, tm), :]
            above = a_scr[s, pl.ds(HALO + 1, tm, stride=1), :]
            return (cb_ref[layer:layer + 1, cols] + below * cw_ref[0:1, cols] + mid * cw_ref[1:2, cols]
                    + above * cw_ref[2:3, cols])

        acts = []
        for s in range(slabs):
            gate = conv(ag_scr, 0, s)
            up = conv(au_scr, d_ff, s)
            acts.append((gate * (1.0 / (1.0 + jnp.exp(-gate))) * up).astype(_BF16))
        o_ref[...] += _dot(jnp.concatenate(acts, axis=1), wd_buf[slot])
        return carry

    lax.fori_loop(0, nj, step, 0)


def _ffn(layer, hn, x2, w_up, conv_w, conv_b, w_down, *, seq):
    tokens, d_model = x2.shape
    d_ff = w_down.shape[0]
    tm, tf = FFN_TM, FFN_TF
    tiles_per_seq = seq // tm
    halo_blocks = tm // HALO
    last_halo = tokens // HALO - 1
    wrows = tm + 2 * HALO
    n_tiles = tokens // tm
    buffers = (3 * tm * d_model * 4 + tm * d_model * 2 + wrows * d_model * 2 + 2 * wrows * tf * 4
               + 2 * 3 * d_model * tf * 2)
    assert buffers < VMEM_LIMIT_BYTES, (buffers, VMEM_LIMIT_BYTES)
    a_scratch = pltpu.VMEM((tf // LANES, wrows, LANES), _F32)
    any_spec = pl.BlockSpec(memory_space=pl.ANY)
    return pl.pallas_call(
        functools.partial(_ffn_body, layer=layer, tiles_per_seq=tiles_per_seq, n_tiles=n_tiles, nj=d_ff // tf),
        grid=(n_tiles,),
        in_specs=[
            pl.BlockSpec((HALO, d_model), lambda m: (jnp.maximum(m * halo_blocks - 1, 0), 0)),
            any_spec,
            pl.BlockSpec((HALO, d_model), lambda m: (jnp.minimum((m + 1) * halo_blocks, last_halo), 0)),
            any_spec,
            any_spec,
            _layer_block(conv_w, layer),
            _resident(conv_b),
            any_spec,
        ],
        out_specs=pl.BlockSpec((tm, d_model), lambda m: (m, 0)),
        out_shape=jax.ShapeDtypeStruct((tokens, d_model), _F32),
        scratch_shapes=[pltpu.VMEM((wrows, d_model), _BF16), a_scratch, a_scratch,
                        pltpu.VMEM((tm, d_model), _BF16), pltpu.VMEM((tm, d_model), _F32),
                        pltpu.VMEM((2, d_model, tf), _BF16), pltpu.VMEM((2, d_model, tf), _BF16),
                        pltpu.VMEM((2, tf, d_model), _BF16),
                        pltpu.SemaphoreType.DMA((2,)), pltpu.SemaphoreType.DMA((3, 2))],
        compiler_params=pltpu.CompilerParams(dimension_semantics=("arbitrary",),
                                             vmem_limit_bytes=VMEM_LIMIT_BYTES),
        name="ffn",
    )(hn, hn, hn, x2, w_up, conv_w, conv_b, w_down)


def _rope_tables(seq):
    inv_freq = ROPE_THETA ** (-np.arange(0, HEAD_DIM, 2, dtype=np.float64) / HEAD_DIM)
    ang = np.arange(seq, dtype=np.float64)[:, None] * inv_freq[None, :]
    cos, sin = np.cos(ang), np.sin(ang)
    table = lambda a, b: jnp.asarray(np.concatenate([a, b], axis=-1).astype(np.float32))
    return table(cos, cos), table(-sin, sin)


def kernel(x, norm1_g, w_in, q_norm_g, k_norm_g, sink, sgu_ln_g, sgu_ln_b, w_s, b_s,
           attn_out_g, sgu_out_g, w_o, norm2_g, w_up, conv_w, conv_b, w_down):
    batch, seq, d_model = x.shape
    depth = w_in.shape[0]
    attn_w = attn_out_g.shape[1]
    gmlp_w = sgu_out_g.shape[1]
    kv_w = (w_in.shape[2] - attn_w - 2 * gmlp_w) // 2
    assert seq % max(IN_TM, MIX_TM, FFN_TM) == 0 and w_down.shape[1] % FFN_TF == 0
    assert w_s.shape[2] == BLOCK and seq >= BAND and MIX_TM % (MIX_GROUP_BLOCKS * BLOCK) == 0

    cosf, sinf = _rope_tables(seq)
    g1, qg, kg, lng, lnb = norm1_g, q_norm_g, k_norm_g, sgu_ln_g, sgu_ln_b
    ag, sg, g2, cb = attn_out_g, sgu_out_g, norm2_g, conv_b
    bst = jnp.swapaxes(b_s, 1, 2)
    ws = w_s.astype(_BF16)
    win = w_in[0].astype(_BF16)

    x2 = x.reshape(batch * seq, d_model)
    for l in range(depth):
        to_cast = [(w_o, l), (w_up, l), (w_down, l)] + ([(w_in, l + 1)] if l + 1 < depth else [])
        q, k, v, u, vn, wo, wup, wdn, *nxt = _in_proj(l, x2, g1, win, qg, kg, cosf, sinf, lng, lnb, to_cast,
                                                      seq=seq, attn_w=attn_w, kv_w=kv_w, gmlp_w=gmlp_w)
        win = nxt[0] if nxt else None
        x2, hn = _mixer(l, sink, q, k, v, u, vn, x2, wo, ws, bst, ag, sg, g2, batch=batch, seq=seq)
        x2 = _ffn(l, hn, x2, wup, conv_w, cb, wdn, seq=seq)
    return x2.reshape(batch, seq, d_model)
```
